```python
import math
import jax, jax.numpy as jnp
from jax import lax
import numpy as np

D_MODEL = 1024
BATCH = 16
SEQ = 2048
DEPTH = 2

GRID_W = 64
CTX_LEN = 256
N_MIXERS = 2
NA_HEADS = 16
NA_HEAD_DIM = D_MODEL // NA_HEADS
NA_WIN_R = 8
NA_WIN_C = 16
NA_QCOLS = 16
NA_KCOLS = NA_QCOLS + NA_WIN_C
DIFF_HEADS = 8
DIFF_HEAD_DIM = 64
DIFF_V_DIM = 2 * DIFF_HEAD_DIM
Q_BLOCK = 128
ROPE_THETA = 10000.0
ROPE_AXIS_DIM = DIFF_HEAD_DIM // 2
D_FF = 2816
N_EXPERTS = 8
TOP_K = 2
D_FF_EXPERT = 3584
NORM_EPS = 1e-6

kernel_name = "hybrid_natten_diffattn_moe_dit"


def rmsnorm(x, g):
    x32 = x.astype(jnp.float32)
    y = x32 * lax.rsqrt(jnp.mean(x32 * x32, axis=-1, keepdims=True) + NORM_EPS)
    return y.astype(x.dtype) * g


def adaln_params(cond, w_ada, b_ada):
    return jnp.split(jax.nn.silu(cond) @ w_ada + b_ada, 6, axis=-1)


def modulate(h, shift, scale):
    return h * (1 + scale) + shift


def swiglu(h, w_gate, w_up, w_down):
    return (jax.nn.silu(h @ w_gate) * (h @ w_up)) @ w_down


def moe_swiglu(h, w_router, w_gate, w_up, w_down):
    logits = (h @ w_router).astype(jnp.float32)
    top_vals, top_idx = lax.top_k(logits, TOP_K)
    top_w = jax.nn.softmax(top_vals, axis=-1)
    gates = jnp.sum(jax.nn.one_hot(top_idx, N_EXPERTS, dtype=jnp.float32) * top_w[..., None], axis=-2)
    gates = gates.astype(h.dtype)
    y = jnp.zeros_like(h)
    for e in range(N_EXPERTS):
        y = y + gates[..., e:e + 1] * swiglu(h, w_gate[e], w_up[e], w_down[e])
    return y


def full_attention(q, k, v):
    s = jnp.einsum('bqhd,bkhd->bhqk', q, k) * (q.shape[-1] ** -0.5)
    p = jax.nn.softmax(s.astype(jnp.float32), axis=-1).astype(v.dtype)
    return jnp.einsum('bhqk,bkhd->bqhd', p, v)


def neighbourhood_tables(rows):
    wr = min(NA_WIN_R, rows)
    nqb = GRID_W // NA_QCOLS
    qcols = np.arange(GRID_W).reshape(nqb, NA_QCOLS)
    kstart = np.clip(NA_QCOLS * np.arange(nqb) - (NA_KCOLS - NA_QCOLS) // 2, 0, GRID_W - NA_KCOLS)
    kcols = kstart[:, None] + np.arange(NA_KCOLS)[None, :]
    wstart = np.clip(qcols - NA_WIN_C // 2, 0, GRID_W - NA_WIN_C)
    kc = kcols[:, None, :]
    mask = (kc >= wstart[..., None]) & (kc < wstart[..., None] + NA_WIN_C)
    mask = np.broadcast_to(mask[:, :, None, :], (nqb, NA_QCOLS, wr, NA_KCOLS)).reshape(nqb, NA_QCOLS, wr * NA_KCOLS)
    dc = np.clip(kc - qcols[..., None], -(NA_WIN_C - 1), NA_WIN_C - 1) + (NA_WIN_C - 1)
    return wr, nqb, kcols, mask, dc


def neighbourhood_attention(h, hc, w_qkv, rpb, w_o, ctx_queries):
    B, S, _ = h.shape
    C = hc.shape[1]
    rows = S // GRID_W
    H, dh = NA_HEADS, NA_HEAD_DIM
    scale = dh ** -0.5
    q, k, v = jnp.split(h @ w_qkv, 3, axis=-1)
    qc, kc, vc = jnp.split(hc @ w_qkv, 3, axis=-1)
    q_grid = q.reshape(B, rows, GRID_W, H, dh)
    k_grid = k.reshape(B, rows, GRID_W, H, dh)
    v_grid = v.reshape(B, rows, GRID_W, H, dh)
    kc = kc.reshape(B, C, H, dh)
    vc = vc.reshape(B, C, H, dh)
    wr, nqb, kcols, mask, dc = neighbourhood_tables(rows)
    n_lat = wr * NA_KCOLS

    def row_step(r):
        rs = jnp.clip(r - wr // 2, 0, rows - wr)
        q_r = lax.dynamic_index_in_dim(q_grid, r, axis=1, keepdims=False).reshape(B, nqb, NA_QCOLS, H, dh)
        k_band = lax.dynamic_slice_in_dim(k_grid, rs, wr, axis=1)[:, :, kcols]
        v_band = lax.dynamic_slice_in_dim(v_grid, rs, wr, axis=1)[:, :, kcols]
        k_blk = jnp.swapaxes(k_band, 1, 2).reshape(B, nqb, n_lat, H, dh)
        v_blk = jnp.swapaxes(v_band, 1, 2).reshape(B, nqb, n_lat, H, dh)
        dr = rs + jnp.arange(wr) - r + (NA_WIN_R - 1)
        bias = jnp.take(rpb[:, dr], dc, axis=2)
        bias = jnp.transpose(bias, (0, 2, 3, 1, 4)).reshape(H, nqb, NA_QCOLS, n_lat)
        s_lat = jnp.einsum('bnqhd,bnkhd->bhnqk', q_r, k_blk) * scale
        s_lat = jnp.where(mask, s_lat.astype(jnp.float32) + bias.astype(jnp.float32), -jnp.inf)
        s_ctx = (jnp.einsum('bnqhd,bkhd->bhnqk', q_r, kc) * scale).astype(jnp.float32)
        p = jax.nn.softmax(jnp.concatenate([s_lat, s_ctx], axis=-1), axis=-1).astype(v.dtype)
        o = (jnp.einsum('bhnqk,bnkhd->bnqhd', p[..., :n_lat], v_blk)
             + jnp.einsum('bhnqk,bkhd->bnqhd', p[..., n_lat:], vc))
        return o.reshape(B, GRID_W, H * dh)

    o = lax.map(row_step, jnp.arange(rows))
    y = jnp.moveaxis(o, 0, 1).reshape(B, S, H * dh) @ w_o
    yc = None
    if ctx_queries:
        oc = full_attention(qc.reshape(B, C, H, dh), kc, vc)
        yc = oc.reshape(B, C, H * dh) @ w_o
    return y, yc


def rope_rotate(x, cos, sin):
    x1, x2 = jnp.split(x, 2, axis=-1)
    return jnp.concatenate([x1 * cos - x2 * sin, x2 * cos + x1 * sin], axis=-1)


def axial_rope(x, rows_pos, cols_pos):
    inv_freq = ROPE_THETA ** (-jnp.arange(0, ROPE_AXIS_DIM, 2, dtype=jnp.float32) / ROPE_AXIS_DIM)
    ang_r = (rows_pos[:, None] * inv_freq)[:, None, None, :]
    ang_c = (cols_pos[:, None] * inv_freq)[:, None, None, :]
    cr, sr = jnp.cos(ang_r).astype(x.dtype), jnp.sin(ang_r).astype(x.dtype)
    cc, sc = jnp.cos(ang_c).astype(x.dtype), jnp.sin(ang_c).astype(x.dtype)
    return jnp.concatenate([rope_rotate(x[..., :ROPE_AXIS_DIM], cr, sr),
                            rope_rotate(x[..., ROPE_AXIS_DIM:], cc, sc)], axis=-1)


def diff_attend(q, k, v, lam):
    s = jnp.einsum('bqhmd,bkhmd->bhmqk', q, k) * (DIFF_HEAD_DIM ** -0.5)
    p = jax.nn.softmax(s.astype(jnp.float32), axis=-1)
    a = (p[:, :, 0] - lam * p[:, :, 1]).astype(v.dtype)
    return jnp.einsum('bhqk,bkhe->bqhe', a, v)


def head_rmsnorm(o, g):
    o32 = o.astype(jnp.float32)
    y = o32 * lax.rsqrt(jnp.mean(o32 * o32, axis=-1, keepdims=True) + NORM_EPS)
    return y.astype(o.dtype) * g


def diff_attention(h, hc, layer_idx, w_qkv, lq1, lk1, lq2, lk2, subln, w_o, ctx_queries):
    B, S, _ = h.shape
    C = hc.shape[1]
    H, dh, E = DIFF_HEADS, DIFF_HEAD_DIM, DIFF_V_DIM
    lam_init = 0.8 - 0.6 * math.exp(-0.3 * layer_idx)
    lam = (jnp.exp(jnp.sum(lq1.astype(jnp.float32) * lk1.astype(jnp.float32)))
           - jnp.exp(jnp.sum(lq2.astype(jnp.float32) * lk2.astype(jnp.float32))) + lam_init)
    q, k, v = jnp.split(h @ w_qkv, 3, axis=-1)
    qc, kc, vc = jnp.split(hc @ w_qkv, 3, axis=-1)
    q = q.reshape(B, S, H, 2, dh)
    k = k.reshape(B, S, H, 2, dh)
    v = v.reshape(B, S, H, E)
    kc = kc.reshape(B, C, H, 2, dh)
    vc = vc.reshape(B, C, H, E)
    pos = jnp.arange(S)
    rows_pos = (pos // GRID_W).astype(jnp.float32)
    cols_pos = (pos % GRID_W).astype(jnp.float32)
    q = axial_rope(q, rows_pos, cols_pos)
    k = axial_rope(k, rows_pos, cols_pos)
    k_all = jnp.concatenate([k, kc], axis=1)
    v_all = jnp.concatenate([v, vc], axis=1)
    nb = S // Q_BLOCK
    q_blocks = jnp.moveaxis(q.reshape(B, nb, Q_BLOCK, H, 2, dh), 1, 0)
    o = lax.map(lambda qb: diff_attend(qb, k_all, v_all, lam), q_blocks)
    o = jnp.moveaxis(o, 0, 1).reshape(B, S, H, E)
    y = (head_rmsnorm(o, subln) * (1 - lam_init)).reshape(B, S, H * E) @ w_o
    yc = None
    if ctx_queries:
        oc = diff_attend(qc.reshape(B, C, H, 2, dh), kc, vc, lam)
        yc = (head_rmsnorm(oc, subln) * (1 - lam_init)).reshape(B, C, H * E) @ w_o
    return y, yc


def setup_inputs(seed: int = 0) -> dict:
    key = jax.random.key(seed)
    ks = iter(jax.random.split(key, 40))
    D = D_MODEL

    def nrm(shape, scale):
        return jax.random.normal(next(ks), shape, jnp.float32) * scale

    def gain(n):
        return 1.0 + nrm((n,), 0.01)

    inp = {}
    inp["x"] = nrm((BATCH, SEQ, D), 1.0)
    inp["c"] = nrm((BATCH, D), 1.0)
    inp["ctx"] = nrm((BATCH, CTX_LEN, D), 1.0)
    inp["c_ctx"] = nrm((D,), 1.0)
    inp["l0_w_ada"] = nrm((D, 6 * D), 0.5 * D ** -0.5)
    inp["l0_b_ada"] = nrm((6 * D,), 0.02)
    inp["l0_norm_mix"] = gain(D)
    inp["l0_w_qkv"] = nrm((D, 3 * D), D ** -0.5)
    inp["l0_rpb"] = nrm((NA_HEADS, 2 * NA_WIN_R - 1, 2 * NA_WIN_C - 1), 0.1)
    inp["l0_w_o"] = nrm((D, D), D ** -0.5)
    inp["l0_norm_ffn"] = gain(D)
    inp["l0_w_gate"] = nrm((D, D_FF), D ** -0.5)
    inp["l0_w_up"] = nrm((D, D_FF), D ** -0.5)
    inp["l0_w_down"] = nrm((D_FF, D), D_FF ** -0.5)
    inp["l1_w_ada"] = nrm((D, 6 * D), 0.5 * D ** -0.5)
    inp["l1_b_ada"] = nrm((6 * D,), 0.02)
    inp["l1_norm_mix"] = gain(D)
    inp["l1_w_qkv"] = nrm((D, 3 * D), D ** -0.5)
    inp["l1_lambda_q1"] = nrm((DIFF_HEAD_DIM,), 0.1)
    inp["l1_lambda_k1"] = nrm((DIFF_HEAD_DIM,), 0.1)
    inp["l1_lambda_q2"] = nrm((DIFF_HEAD_DIM,), 0.1)
    inp["l1_lambda_k2"] = nrm((DIFF_HEAD_DIM,), 0.1)
    inp["l1_subln"] = gain(DIFF_V_DIM)
    inp["l1_w_o"] = nrm((D, D), D ** -0.5)
    inp["l1_norm_ffn"] = gain(D)
    inp["l1_w_router"] = nrm((D, N_EXPERTS), D ** -0.5)
    inp["l1_w_gate"] = nrm((N_EXPERTS, D, D_FF_EXPERT), D ** -0.5)
    inp["l1_w_up"] = nrm((N_EXPERTS, D, D_FF_EXPERT), D ** -0.5)
    inp["l1_w_down"] = nrm((N_EXPERTS, D_FF_EXPERT, D), D_FF_EXPERT ** -0.5)
    inp["final_norm"] = gain(D)
    return inp


def reference(x, c, ctx, c_ctx,
              l0_w_ada, l0_b_ada, l0_norm_mix, l0_w_qkv, l0_rpb, l0_w_o,
              l0_norm_ffn, l0_w_gate, l0_w_up, l0_w_down,
              l1_w_ada, l1_b_ada, l1_norm_mix, l1_w_qkv,
              l1_lambda_q1, l1_lambda_k1, l1_lambda_q2, l1_lambda_k2, l1_subln, l1_w_o,
              l1_norm_ffn, l1_w_router, l1_w_gate, l1_w_up, l1_w_down,
              final_norm):
    layers = [
        dict(w_ada=l0_w_ada, b_ada=l0_b_ada, norm_mix=l0_norm_mix, w_qkv=l0_w_qkv, rpb=l0_rpb,
             w_o=l0_w_o, norm_ffn=l0_norm_ffn, w_gate=l0_w_gate, w_up=l0_w_up, w_down=l0_w_down),
        dict(w_ada=l1_w_ada, b_ada=l1_b_ada, norm_mix=l1_norm_mix, w_qkv=l1_w_qkv,
             lq1=l1_lambda_q1, lk1=l1_lambda_k1, lq2=l1_lambda_q2, lk2=l1_lambda_k2, subln=l1_subln,
             w_o=l1_w_o, norm_ffn=l1_norm_ffn, w_router=l1_w_router,
             w_gate=l1_w_gate, w_up=l1_w_up, w_down=l1_w_down),
    ]
    for i in range(DEPTH):
        p = layers[i]
        ctx_needed = i < DEPTH - 1
        sh1, sc1, g1, sh2, sc2, g2 = adaln_params(c[:, None, :], p["w_ada"], p["b_ada"])
        csh1, csc1, cg1, csh2, csc2, cg2 = adaln_params(c_ctx, p["w_ada"], p["b_ada"])
        h = modulate(rmsnorm(x, p["norm_mix"]), sh1, sc1)
        hc = modulate(rmsnorm(ctx, p["norm_mix"]), csh1, csc1)
        if i % N_MIXERS == 0:
            y, yc = neighbourhood_attention(h, hc, p["w_qkv"], p["rpb"], p["w_o"], ctx_needed)
        else:
            y, yc = diff_attention(h, hc, i, p["w_qkv"], p["lq1"], p["lk1"], p["lq2"], p["lk2"],
                                   p["subln"], p["w_o"], ctx_needed)
        x = x + g1 * y
        if i % 2 == 0:
            ffn = lambda t: swiglu(t, p["w_gate"], p["w_up"], p["w_down"])
        else:
            ffn = lambda t: moe_swiglu(t, p["w_router"], p["w_gate"], p["w_up"], p["w_down"])
        x = x + g2 * ffn(modulate(rmsnorm(x, p["norm_ffn"]), sh2, sc2))
        if ctx_needed:
            ctx = ctx + cg1 * yc
            ctx = ctx + cg2 * ffn(modulate(rmsnorm(ctx, p["norm_ffn"]), csh2, csc2))
    return rmsnorm(x, final_norm)
```

```python
import functools
import math

import jax
import jax.numpy as jnp
from jax import lax
from jax.experimental import pallas as pl
from jax.experimental.pallas import tpu as pltpu

F32 = jnp.float32
BF16 = jnp.bfloat16

GRID_W = 64
NA_HEADS = 16
NA_WIN_R = 8
NA_WIN_C = 16
NA_QROWS = 8
NA_KROWS = 16
DIFF_HEADS = 8
DIFF_HEAD_DIM = 64
ROPE_THETA = 10000.0
TOP_K = 2
NORM_EPS = 1e-6
HEAD_LANES = 128
MASK_VALUE = -1e30
N_MOD = 6
VMEM_LIMIT = 56 * 1024 * 1024


def _cparams(n_axes, vmem=VMEM_LIMIT):
    return pltpu.CompilerParams(dimension_semantics=("arbitrary",) * n_axes, vmem_limit_bytes=vmem)


def _rms(x):
    return x * lax.rsqrt(jnp.mean(x * x, axis=-1, keepdims=True) + NORM_EPS)


def _ada_kernel(c_ref, w_ref, b_ref, o_ref):
    c = c_ref[...]
    s = c * jax.nn.sigmoid(c)
    o_ref[...] = jnp.dot(s, w_ref[...], preferred_element_type=F32,
                         precision=lax.Precision.HIGHEST) + b_ref[...]


def ada_params(cond, w_ada, b_ada):
    R, D = cond.shape
    N = w_ada.shape[1]
    tn = N // 4
    out = pl.pallas_call(
        _ada_kernel,
        grid=(N // tn,),
        in_specs=[pl.BlockSpec((R, D), lambda j: (0, 0)),
                  pl.BlockSpec((D, tn), lambda j: (0, j)),
                  pl.BlockSpec((1, tn), lambda j: (0, j))],
        out_specs=pl.BlockSpec((R, tn), lambda j: (0, j)),
        out_shape=jax.ShapeDtypeStruct((R, N), F32),
        compiler_params=_cparams(1),
        name="ada_params",
    )(cond, w_ada, b_ada.reshape(1, N))
    return out.reshape(R * N_MOD, 1, D)


def _qkv_kernel(*refs, rope, d_model):
    if rope:
        x_ref, g_ref, sh_ref, sc_ref, w_ref, cos_ref, sa_ref, sb_ref, o_ref = refs
    else:
        x_ref, g_ref, sh_ref, sc_ref, w_ref, o_ref = refs
    h = (_rms(x_ref[...]) * g_ref[...]) * (1.0 + sc_ref[0]) + sh_ref[0]
    hb = h.astype(BF16)
    n_chunks = w_ref.shape[1] // d_model
    for n in range(n_chunks):
        acc = jnp.dot(hb, w_ref[:, n * d_model:(n + 1) * d_model], preferred_element_type=F32)
        if rope and n < 2:
            cos, sa, sb = cos_ref[...], sa_ref[...], sb_ref[...]
            for s in range(d_model // HEAD_LANES):
                xs = acc[:, s * HEAD_LANES:(s + 1) * HEAD_LANES]
                rot = xs * cos + pltpu.roll(xs, HEAD_LANES - 16, 1) * sa + pltpu.roll(xs, 16, 1) * sb
                col = n * d_model + s * HEAD_LANES
                o_ref[:, col:col + HEAD_LANES] = rot.astype(BF16)
        else:
            o_ref[:, n * d_model:(n + 1) * d_model] = acc.astype(BF16)


def norm_mod_qkv(x, gain, mod, mod_base, rows_per_group, w_bf16, tm, rope_tables=None):
    rows, D = x.shape
    N = w_bf16.shape[1]
    tiles_per_group = rows_per_group // tm
    in_specs = [pl.BlockSpec((tm, D), lambda i: (i, 0)),
                pl.BlockSpec((1, D), lambda i: (0, 0)),
                pl.BlockSpec((1, 1, D), lambda i: (mod_base + (i // tiles_per_group) * N_MOD + 0, 0, 0)),
                pl.BlockSpec((1, 1, D), lambda i: (mod_base + (i // tiles_per_group) * N_MOD + 1, 0, 0)),
                pl.BlockSpec((D, N), lambda i: (0, 0))]
    args = [x, gain.reshape(1, D), mod, mod, w_bf16]
    if rope_tables is not None:
        seq_tiles = rope_tables[0].shape[0] // tm
        for t in rope_tables:
            in_specs.append(pl.BlockSpec((tm, HEAD_LANES), lambda i: (i % seq_tiles, 0)))
            args.append(t)
    return pl.pallas_call(
        functools.partial(_qkv_kernel, rope=rope_tables is not None, d_model=D),
        grid=(rows // tm,),
        in_specs=in_specs,
        out_specs=pl.BlockSpec((tm, N), lambda i: (i, 0)),
        out_shape=jax.ShapeDtypeStruct((rows, N), BF16),
        compiler_params=_cparams(1),
        name="norm_mod_qkv",
    )(*args)


def rope_tables(seq):
    half = DIFF_HEAD_DIM // 2
    n_freq = half // 2
    inv_freq = ROPE_THETA ** (-jnp.arange(0, half, 2, dtype=F32) / half)
    pos = jnp.arange(seq)
    rows_pos = (pos // GRID_W).astype(F32)
    cols_pos = (pos % GRID_W).astype(F32)
    lane = jnp.arange(HEAD_LANES)
    d = lane % DIFF_HEAD_DIM
    use_col = (d // half) == 1
    j = d % half
    freq = inv_freq[j % n_freq]
    ang = jnp.where(use_col[None, :], cols_pos[:, None], rows_pos[:, None]) * freq[None, :]
    cos, sin = jnp.cos(ang), jnp.sin(ang)
    first = (j < n_freq)[None, :]
    return cos, jnp.where(first, -sin, 0.0), jnp.where(first, 0.0, sin)


def _head_mask(q, head):
    lane = lax.broadcasted_iota(jnp.int32, (1, HEAD_LANES), 1)
    return jnp.where((lane // 64) == head, q, jnp.zeros_like(q))


def _qk(q, k):
    return lax.dot_general(q, k, (((1,), (1,)), ((), ())), preferred_element_type=F32)


def _softmax2(s_a, s_b):
    m = jnp.maximum(jnp.max(s_a, axis=-1, keepdims=True), jnp.max(s_b, axis=-1, keepdims=True))
    p_a = jnp.exp(s_a - m)
    p_b = jnp.exp(s_b - m)
    denom = jnp.sum(p_a, axis=-1, keepdims=True) + jnp.sum(p_b, axis=-1, keepdims=True)
    return p_a, p_b, 1.0 / denom


def _na_kernel(q_ref, k_ref, v_ref, kc_ref, vc_ref, bias_ref, o_ref, *, grid_rows):
    g = pl.program_id(0)
    k_row0 = jnp.clip(g * NA_QROWS - NA_WIN_R // 2, 0, grid_rows - NA_KROWS) * GRID_W
    k_row0 = pl.multiple_of(k_row0, GRID_W)
    n_keys = NA_KROWS * GRID_W
    q = q_ref[...]
    k_lat = k_ref[pl.ds(k_row0, n_keys), :]
    v_lat = v_ref[pl.ds(k_row0, n_keys), :]
    kc, vc = kc_ref[...], vc_ref[...]
    outs = []
    for head in range(2):
        qm = _head_mask(q, head)
        s_lat = _qk(qm, k_lat) + bias_ref[0, head]
        s_ctx = _qk(qm, kc)
        p_lat, p_ctx, inv = _softmax2(s_lat, s_ctx)
        o = (jnp.dot(p_lat.astype(BF16), v_lat, preferred_element_type=F32)
             + jnp.dot(p_ctx.astype(BF16), vc, preferred_element_type=F32))
        outs.append(o * inv)
    lane = lax.broadcasted_iota(jnp.int32, (1, HEAD_LANES), 1)
    o_ref[...] = jnp.where(lane < 64, outs[0], outs[1]).astype(BF16)


def na_bias(rpb, grid_rows):
    n_groups = grid_rows // NA_QROWS
    g = jnp.arange(n_groups)[:, None, None]
    qr = jnp.arange(NA_QROWS)[None, :, None]
    kr = jnp.arange(NA_KROWS)[None, None, :]
    r = g * NA_QROWS + qr
    k_abs = jnp.clip(g * NA_QROWS - NA_WIN_R // 2, 0, grid_rows - NA_KROWS) + kr
    rs = jnp.clip(r - NA_WIN_R // 2, 0, grid_rows - NA_WIN_R)
    row_ok = (k_abs >= rs) & (k_abs < rs + NA_WIN_R)
    dr = jnp.clip(k_abs - r + (NA_WIN_R - 1), 0, 2 * NA_WIN_R - 2)
    qc = jnp.arange(GRID_W)[:, None]
    kcol = jnp.arange(GRID_W)[None, :]
    ws = jnp.clip(qc - NA_WIN_C // 2, 0, GRID_W - NA_WIN_C)
    col_ok = (kcol >= ws) & (kcol < ws + NA_WIN_C)
    dc = jnp.clip(kcol - qc, -(NA_WIN_C - 1), NA_WIN_C - 1) + (NA_WIN_C - 1)
    by_row = rpb[:, dr]
    full = by_row[..., dc]
    ok = row_ok[None, :, :, :, None, None] & col_ok[None, None, None, None]
    full = jnp.where(ok, full, MASK_VALUE)
    full = jnp.transpose(full, (1, 0, 2, 4, 3, 5))
    H = rpb.shape[0]
    return full.reshape(n_groups, H, NA_QROWS * GRID_W, NA_KROWS * GRID_W)


def neighbourhood_attention(qkv, qkv_ctx, bias, batch, seq, ctx_len, d_model):
    grid_rows = seq // GRID_W
    n_groups = grid_rows // NA_QROWS
    tq = NA_QROWS * GRID_W
    n_hp = d_model // HEAD_LANES
    nk = NA_KROWS * GRID_W
    return pl.pallas_call(
        functools.partial(_na_kernel, grid_rows=grid_rows),
        grid=(n_groups, n_hp, batch),
        in_specs=[pl.BlockSpec((tq, HEAD_LANES), lambda g, h, b: (b * n_groups + g, h)),
                  pl.BlockSpec((seq, HEAD_LANES), lambda g, h, b: (b, n_hp + h)),
                  pl.BlockSpec((seq, HEAD_LANES), lambda g, h, b: (b, 2 * n_hp + h)),
                  pl.BlockSpec((ctx_len, HEAD_LANES), lambda g, h, b: (b, n_hp + h)),
                  pl.BlockSpec((ctx_len, HEAD_LANES), lambda g, h, b: (b, 2 * n_hp + h)),
                  pl.BlockSpec((1, 2, tq, nk), lambda g, h, b: (g, h, 0, 0))],
        out_specs=pl.BlockSpec((tq, HEAD_LANES), lambda g, h, b: (b * n_groups + g, h)),
        out_shape=jax.ShapeDtypeStruct((batch * seq, d_model), BF16),
        compiler_params=_cparams(3),
        name="neighbourhood_attention",
    )(qkv, qkv, qkv, qkv_ctx, qkv_ctx, bias)


def _ctx_attn_kernel(q_ref, k_ref, v_ref, o_ref):
    q, k, v = q_ref[...], k_ref[...], v_ref[...]
    outs = []
    for head in range(2):
        s = _qk(_head_mask(q, head), k)
        m = jnp.max(s, axis=-1, keepdims=True)
        p = jnp.exp(s - m)
        inv = 1.0 / jnp.sum(p, axis=-1, keepdims=True)
        outs.append(jnp.dot(p.astype(BF16), v, preferred_element_type=F32) * inv)
    lane = lax.broadcasted_iota(jnp.int32, (1, HEAD_LANES), 1)
    o_ref[...] = jnp.where(lane < 64, outs[0], outs[1]).astype(BF16)


def ctx_attention(qkv_ctx, batch, ctx_len, d_model):
    n_hp = d_model // HEAD_LANES
    return pl.pallas_call(
        _ctx_attn_kernel,
        grid=(batch, n_hp),
        in_specs=[pl.BlockSpec((ctx_len, HEAD_LANES), lambda b, h: (b, h)),
                  pl.BlockSpec((ctx_len, HEAD_LANES), lambda b, h: (b, n_hp + h)),
                  pl.BlockSpec((ctx_len, HEAD_LANES), lambda b, h: (b, 2 * n_hp + h))],
        out_specs=pl.BlockSpec((ctx_len, HEAD_LANES), lambda b, h: (b, h)),
        out_shape=jax.ShapeDtypeStruct((batch * ctx_len, d_model), BF16),
        compiler_params=_cparams(2),
        name="ctx_attention",
    )(qkv_ctx, qkv_ctx, qkv_ctx)


def _diff_attn_kernel(q_ref, k_ref, v_ref, kc_ref, vc_ref, lam_ref, subln_ref, o_ref, *, lam_init):
    lp = lam_ref[...]
    lam = (jnp.exp(jnp.sum(lp[0:1] * lp[1:2], axis=-1, keepdims=True))
           - jnp.exp(jnp.sum(lp[2:3] * lp[3:4], axis=-1, keepdims=True)) + lam_init)
    q = q_ref[...]
    k, v, kc, vc = k_ref[...], v_ref[...], kc_ref[...], vc_ref[...]
    q1 = _head_mask(q, 0)
    q2 = _head_mask(q, 1)
    p1_lat, p1_ctx, inv1 = _softmax2(_qk(q1, k), _qk(q1, kc))
    p2_lat, p2_ctx, inv2 = _softmax2(_qk(q2, k), _qk(q2, kc))
    w2 = lam * inv2
    a_lat = (p1_lat * inv1 - p2_lat * w2).astype(BF16)
    a_ctx = (p1_ctx * inv1 - p2_ctx * w2).astype(BF16)
    o = (jnp.dot(a_lat, v, preferred_element_type=F32)
         + jnp.dot(a_ctx, vc, preferred_element_type=F32))
    o_ref[...] = ((_rms(o) * subln_ref[...]) * (1.0 - lam_init)).astype(BF16)


def diff_attention(qkv, qkv_ctx, lam_params, subln, lam_init, batch, seq, ctx_len, d_model, tq):
    n_h = d_model // HEAD_LANES
    nq = seq // tq
    return pl.pallas_call(
        functools.partial(_diff_attn_kernel, lam_init=lam_init),
        grid=(batch, n_h, nq),
        in_specs=[pl.BlockSpec((tq, HEAD_LANES), lambda b, h, i: (b * nq + i, h)),
                  pl.BlockSpec((seq, HEAD_LANES), lambda b, h, i: (b, n_h + h)),
                  pl.BlockSpec((seq, HEAD_LANES), lambda b, h, i: (b, 2 * n_h + h)),
                  pl.BlockSpec((ctx_len, HEAD_LANES), lambda b, h, i: (b, n_h + h)),
                  pl.BlockSpec((ctx_len, HEAD_LANES), lambda b, h, i: (b, 2 * n_h + h)),
                  pl.BlockSpec(lam_params.shape, lambda b, h, i: (0, 0)),
                  pl.BlockSpec((1, HEAD_LANES), lambda b, h, i: (0, 0))],
        out_specs=pl.BlockSpec((tq, HEAD_LANES), lambda b, h, i: (b * nq + i, h)),
        out_shape=jax.ShapeDtypeStruct((batch * seq, d_model), BF16),
        compiler_params=_cparams(3),
        name="diff_attention",
    )(qkv, qkv, qkv, qkv_ctx, qkv_ctx, lam_params, subln.reshape(1, HEAD_LANES))


def _oproj_kernel(*refs, route, n_experts):
    if route:
        o_ref, w_ref, x_ref, g1_ref, gain_ref, sh_ref, sc_ref, wr_ref, xo_ref, h_ref, r_ref = refs
    else:
        o_ref, w_ref, x_ref, g1_ref, gain_ref, sh_ref, sc_ref, xo_ref, h_ref = refs
    y = jnp.dot(o_ref[...], w_ref[...], preferred_element_type=F32)
    xn = x_ref[...] + g1_ref[0] * y
    xo_ref[...] = xn
    h = (_rms(xn) * gain_ref[...]) * (1.0 + sc_ref[0]) + sh_ref[0]
    if not route:
        h_ref[...] = h.astype(BF16)
        return
    h_ref[...] = h
    n_exp = n_experts
    logits = jnp.dot(h, wr_ref[...], preferred_element_type=F32, precision=lax.Precision.HIGHEST)
    lane = lax.broadcasted_iota(jnp.int32, logits.shape, 1)
    logits = jnp.where(lane < n_exp, logits, -jnp.inf)
    v1 = jnp.max(logits, axis=-1, keepdims=True)
    i1 = jnp.min(jnp.where(logits == v1, lane, n_exp), axis=-1, keepdims=True)
    rest = jnp.where(lane == i1, -jnp.inf, logits)
    v2 = jnp.max(rest, axis=-1, keepdims=True)
    i2 = jnp.min(jnp.where(rest == v2, lane, n_exp), axis=-1, keepdims=True)
    e = jnp.exp(v2 - v1)
    w1 = 1.0 / (1.0 + e)
    w2 = e * w1
    out_lane = lax.broadcasted_iota(jnp.int32, r_ref.shape, 1)
    r_ref[...] = jnp.where(out_lane == 0, i1.astype(F32),
                           jnp.where(out_lane == 1, i2.astype(F32),
                                     jnp.where(out_lane == 2, w1, jnp.where(out_lane == 3, w2, 0.0))))


def oproj_residual_norm(o, w_o_bf16, x, gain, mod, mod_base, rows_per_group, tm, w_router=None, n_experts=0):
    rows, D = x.shape
    tiles_per_group = rows_per_group // tm

    def mod_spec(k):
        return pl.BlockSpec((1, 1, D), lambda i: (mod_base + (i // tiles_per_group) * N_MOD + k, 0, 0))

    in_specs = [pl.BlockSpec((tm, D), lambda i: (i, 0)),
                pl.BlockSpec((D, D), lambda i: (0, 0)),
                pl.BlockSpec((tm, D), lambda i: (i, 0)),
                mod_spec(2),
                pl.BlockSpec((1, D), lambda i: (0, 0)),
                mod_spec(3), mod_spec(4)]
    args = [o, w_o_bf16, x, mod, gain.reshape(1, D), mod, mod]
    out_specs = [pl.BlockSpec((tm, D), lambda i: (i, 0)), pl.BlockSpec((tm, D), lambda i: (i, 0))]
    out_shape = [jax.ShapeDtypeStruct((rows, D), F32),
                 jax.ShapeDtypeStruct((rows, D), BF16 if w_router is None else F32)]
    if w_router is not None:
        in_specs.append(pl.BlockSpec(w_router.shape, lambda i: (0, 0)))
        args.append(w_router)
        out_specs.append(pl.BlockSpec((tm, HEAD_LANES), lambda i: (i, 0)))
        out_shape.append(jax.ShapeDtypeStruct((rows, HEAD_LANES), F32))
    return pl.pallas_call(
        functools.partial(_oproj_kernel, route=w_router is not None, n_experts=n_experts),
        grid=(rows // tm,),
        in_specs=in_specs,
        out_specs=out_specs,
        out_shape=out_shape,
        compiler_params=_cparams(1),
        name="oproj_residual_norm",
    )(*args)


def _swiglu_chunk(h, wg, wu, wd):
    a = jnp.dot(h, wg, preferred_element_type=F32)
    b = jnp.dot(h, wu, preferred_element_type=F32)
    t = (a * jax.nn.sigmoid(a)) * b
    return jnp.dot(t.astype(BF16), wd, preferred_element_type=F32)


def _ffn_kernel(h_ref, wg_ref, wu_ref, wd_ref, x_ref, g2_ref, o_ref, acc_ref):
    j = pl.program_id(1)

    @pl.when(j == 0)
    def _():
        acc_ref[...] = jnp.zeros_like(acc_ref)

    acc_ref[...] += _swiglu_chunk(h_ref[...], wg_ref[...], wu_ref[...], wd_ref[...])

    @pl.when(j == pl.num_programs(1) - 1)
    def _():
        o_ref[...] = x_ref[...] + g2_ref[0] * acc_ref[...]


def ffn_residual(h, wg, wu, wd, x, mod, mod_base, rows_per_group, tm, tf):
    rows, D = x.shape
    F = wg.shape[1]
    tiles_per_group = rows_per_group // tm
    return pl.pallas_call(
        _ffn_kernel,
        grid=(rows // tm, F // tf),
        in_specs=[pl.BlockSpec((tm, D), lambda i, j: (i, 0)),
                  pl.BlockSpec((D, tf), lambda i, j: (0, j)),
                  pl.BlockSpec((D, tf), lambda i, j: (0, j)),
                  pl.BlockSpec((tf, D), lambda i, j: (j, 0)),
                  pl.BlockSpec((tm, D), lambda i, j: (i, 0)),
                  pl.BlockSpec((1, 1, D), lambda i, j: (mod_base + (i // tiles_per_group) * N_MOD + 5, 0, 0))],
        out_specs=pl.BlockSpec((tm, D), lambda i, j: (i, 0)),
        out_shape=jax.ShapeDtypeStruct((rows, D), F32),
        scratch_shapes=[pltpu.VMEM((tm, D), F32)],
        compiler_params=_cparams(2),
        name="ffn_residual",
    )(h, wg, wu, wd, x, mod)


def _gather_rows_kernel(idx_ref, src_ref, o_ref, sem):
    n = o_ref.shape[0]

    def row_copy(r):
        return pltpu.make_async_copy(src_ref.at[pl.ds(idx_ref[r], 1)], o_ref.at[pl.ds(r, 1)], sem)

    def start(r, c):
        row_copy(r).start()
        return c

    def wait(r, c):
        row_copy(r).wait()
        return c

    lax.fori_loop(0, n, start, 0)
    lax.fori_loop(0, n, wait, 0)


def gather_rows(src, idx, tg):
    n = idx.shape[0]
    D = src.shape[1]
    return pl.pallas_call(
        _gather_rows_kernel,
        grid=(n // tg,),
        in_specs=[pl.BlockSpec((tg,), lambda i: (i,), memory_space=pltpu.SMEM),
                  pl.BlockSpec(memory_space=pl.ANY)],
        out_specs=pl.BlockSpec((tg, D), lambda i: (i, 0)),
        out_shape=jax.ShapeDtypeStruct((n, D), src.dtype),
        scratch_shapes=[pltpu.SemaphoreType.DMA(())],
        compiler_params=_cparams(1),
        name="gather_rows",
    )(idx, src)


def _expert_ffn_kernel(te_ref, valid_ref, h_ref, wg_ref, wu_ref, wd_ref, o_ref, hb_ref, acc_ref):
    i, j = pl.program_id(0), pl.program_id(1)
    last = pl.num_programs(1) - 1

    @pl.when(j == 0)
    def _():
        acc_ref[...] = jnp.zeros_like(acc_ref)
        hb_ref[...] = h_ref[...].astype(BF16)

    @pl.when(valid_ref[i] > 0)
    def _():
        acc_ref[...] += _swiglu_chunk(hb_ref[...], wg_ref[...], wu_ref[...], wd_ref[...])

    @pl.when(j == last)
    def _():
        o_ref[...] = acc_ref[...]


def expert_ffn(h_sorted, tile_expert, tile_valid, wg, wu, wd, tm, tf):
    rows, D = h_sorted.shape
    F = wg.shape[2]
    grid_spec = pltpu.PrefetchScalarGridSpec(
        num_scalar_prefetch=2,
        grid=(rows // tm, F // tf),
        in_specs=[pl.BlockSpec((tm, D), lambda i, j, te, va: (i, 0)),
                  pl.BlockSpec((None, D, tf), lambda i, j, te, va: (te[i], 0, j)),
                  pl.BlockSpec((None, D, tf), lambda i, j, te, va: (te[i], 0, j)),
                  pl.BlockSpec((None, tf, D), lambda i, j, te, va: (te[i], j, 0))],
        out_specs=pl.BlockSpec((tm, D), lambda i, j, te, va: (i, 0)),
        scratch_shapes=[pltpu.VMEM((tm, D), BF16), pltpu.VMEM((tm, D), F32)],
    )
    return pl.pallas_call(
        _expert_ffn_kernel,
        grid_spec=grid_spec,
        out_shape=jax.ShapeDtypeStruct((rows, D), F32),
        compiler_params=_cparams(2),
        name="expert_ffn",
    )(tile_expert, tile_valid, h_sorted, wg, wu, wd)


def _combine_kernel(p1_ref, p2_ref, y_ref, route_ref, x_ref, g2_ref, gain_ref, o_ref, buf_ref, sem):
    n = o_ref.shape[0]

    def row_copy(slot, idx_ref, r):
        return pltpu.make_async_copy(y_ref.at[pl.ds(idx_ref[r], 1)], buf_ref.at[slot, pl.ds(r, 1)], sem)

    def start(r, c):
        row_copy(0, p1_ref, r).start()
        row_copy(1, p2_ref, r).start()
        return c

    def wait(r, c):
        row_copy(0, p1_ref, r).wait()
        row_copy(1, p2_ref, r).wait()
        return c

    lax.fori_loop(0, n, start, 0)
    lax.fori_loop(0, n, wait, 0)
    route = route_ref[...]
    y = route[:, 2:3] * buf_ref[0] + route[:, 3:4] * buf_ref[1]
    xn = x_ref[...] + g2_ref[0] * y
    o_ref[...] = _rms(xn) * gain_ref[...]


def combine_final(y_sorted, pos1, pos2, route, x, mod, mod_base, rows_per_group, final_gain, tm):
    rows, D = x.shape
    tiles_per_group = rows_per_group // tm
    return pl.pallas_call(
        _combine_kernel,
        grid=(rows // tm,),
        in_specs=[pl.BlockSpec((tm,), lambda i: (i,), memory_space=pltpu.SMEM),
                  pl.BlockSpec((tm,), lambda i: (i,), memory_space=pltpu.SMEM),
                  pl.BlockSpec(memory_space=pl.ANY),
                  pl.BlockSpec((tm, HEAD_LANES), lambda i: (i, 0)),
                  pl.BlockSpec((tm, D), lambda i: (i, 0)),
                  pl.BlockSpec((1, 1, D), lambda i: (mod_base + (i // tiles_per_group) * N_MOD + 5, 0, 0)),
                  pl.BlockSpec((1, D), lambda i: (0, 0))],
        out_specs=pl.BlockSpec((tm, D), lambda i: (i, 0)),
        out_shape=jax.ShapeDtypeStruct((rows, D), F32),
        scratch_shapes=[pltpu.VMEM((2, tm, D), F32), pltpu.SemaphoreType.DMA(())],
        compiler_params=_cparams(1),
        name="combine_final",
    )(pos1, pos2, y_sorted, route, x, mod, final_gain.reshape(1, D))


def routing_plan(route, n_experts, tm):
    T = route.shape[0]
    e = jnp.concatenate([route[:, 0], route[:, 1]]).astype(jnp.int32)
    onehot = (e[:, None] == jnp.arange(n_experts)[None, :]).astype(jnp.int32)
    csum = jnp.cumsum(onehot, axis=0)
    rank = jnp.sum((csum - onehot) * onehot, axis=1)
    count = csum[-1]
    padded = ((count + tm - 1) // tm) * tm
    ends = jnp.cumsum(padded)
    offset = ends - padded
    pos = jnp.sum(onehot * offset[None, :], axis=1) + rank
    n_tiles = (TOP_K * T) // tm + n_experts
    tile_start = jnp.arange(n_tiles) * tm
    tile_valid = (tile_start < ends[-1]).astype(jnp.int32)
    tile_expert = jnp.sum((tile_start[:, None] >= ends[None, :]).astype(jnp.int32), axis=1)
    last_expert = jnp.max(jnp.where(tile_valid > 0, tile_expert, 0))
    tile_expert = jnp.where(tile_valid > 0, tile_expert, last_expert)
    src = jnp.zeros((n_tiles * tm,), jnp.int32).at[pos].set(jnp.arange(TOP_K * T, dtype=jnp.int32) % T)
    return src, pos[:T], pos[T:], tile_expert, tile_valid


def kernel(x, c, ctx, c_ctx, l0_w_ada, l0_b_ada, l0_norm_mix, l0_w_qkv, l0_rpb, l0_w_o, l0_norm_ffn, l0_w_gate, l0_w_up, l0_w_down, l1_w_ada, l1_b_ada, l1_norm_mix, l1_w_qkv, l1_lambda_q1, l1_lambda_k1, l1_lambda_q2, l1_lambda_k2, l1_subln, l1_w_o, l1_norm_ffn, l1_w_router, l1_w_gate, l1_w_up, l1_w_down, final_norm):
    B, S, D = x.shape
    C = ctx.shape[1]
    T, TC = B * S, B * C
    n_experts = l1_w_router.shape[1]
    tm = min(512, S)
    tmc = min(512, TC)

    xf = x.reshape(T, D)
    cf = ctx.reshape(TC, D)

    n_cond = ((B + 1 + 7) // 8) * 8
    cond = jnp.zeros((n_cond, D), F32).at[:B].set(c).at[B].set(c_ctx)
    ctx_base = B * N_MOD

    def qkv_weight(w, scale):
        col_scale = jnp.concatenate([jnp.full((D,), scale, F32), jnp.ones((2 * D,), F32)])
        return (w * col_scale[None, :]).astype(BF16)

    mod0 = ada_params(cond, l0_w_ada, l0_b_ada)
    w_qkv0 = qkv_weight(l0_w_qkv, (D // NA_HEADS) ** -0.5)
    qkv = norm_mod_qkv(xf, l0_norm_mix, mod0, 0, S, w_qkv0, tm)
    qkv_c = norm_mod_qkv(cf, l0_norm_mix, mod0, ctx_base, TC, w_qkv0, tmc)
    bias = na_bias(l0_rpb, S // GRID_W)
    o = neighbourhood_attention(qkv, qkv_c, bias, B, S, C, D)
    oc = ctx_attention(qkv_c, B, C, D)
    w_o0 = l0_w_o.astype(BF16)
    xf, h = oproj_residual_norm(o, w_o0, xf, l0_norm_ffn, mod0, 0, S, tm)
    cf, hc = oproj_residual_norm(oc, w_o0, cf, l0_norm_ffn, mod0, ctx_base, TC, tmc)
    wg0, wu0, wd0 = l0_w_gate.astype(BF16), l0_w_up.astype(BF16), l0_w_down.astype(BF16)
    d_ff = l0_w_gate.shape[1]
    tf0 = d_ff // 2 if (d_ff // 2) % 128 == 0 else d_ff
    xf = ffn_residual(h, wg0, wu0, wd0, xf, mod0, 0, S, tm, tf0)
    cf = ffn_residual(hc, wg0, wu0, wd0, cf, mod0, ctx_base, TC, tmc, tf0)

    mod1 = ada_params(cond, l1_w_ada, l1_b_ada)
    w_qkv1 = qkv_weight(l1_w_qkv, DIFF_HEAD_DIM ** -0.5)
    qkv = norm_mod_qkv(xf, l1_norm_mix, mod1, 0, S, w_qkv1, tm, rope_tables=rope_tables(S))
    qkv_c = norm_mod_qkv(cf, l1_norm_mix, mod1, ctx_base, TC, w_qkv1, tmc)
    lam_init = 0.8 - 0.6 * math.exp(-0.3 * 1)
    lam_params = jnp.stack([l1_lambda_q1, l1_lambda_k1, l1_lambda_q2, l1_lambda_k2]).astype(F32)
    o = diff_attention(qkv, qkv_c, lam_params, l1_subln, lam_init, B, S, C, D, tq=min(256, S))
    w_router = jnp.zeros((D, HEAD_LANES), F32).at[:, :n_experts].set(l1_w_router)
    w_router = jnp.where(jnp.arange(HEAD_LANES)[None, :] < n_experts, w_router, 0.0)
    xf, h, route = oproj_residual_norm(o, l1_w_o.astype(BF16), xf, l1_norm_ffn, mod1, 0, S, tm,
                                       w_router=w_router, n_experts=n_experts)
    tme = min(512, S)
    src, pos1, pos2, tile_expert, tile_valid = routing_plan(route, n_experts, tme)
    h_sorted = gather_rows(h, src, tme)
    d_ffe = l1_w_gate.shape[2]
    tfe = 512 if d_ffe % 512 == 0 else d_ffe
    y_sorted = expert_ffn(h_sorted, tile_expert, tile_valid, l1_w_gate.astype(BF16), l1_w_up.astype(BF16),
                          l1_w_down.astype(BF16), tme, tfe)
    out = combine_final(y_sorted, pos1, pos2, route, xf, mod1, 0, S, final_norm, min(256, S))
    return out.reshape(B, S, D)
```

```python
import functools
import math

import jax
import jax.numpy as jnp
from jax import lax
from jax.experimental import pallas as pl
from jax.experimental.pallas import tpu as pltpu

F32 = jnp.float32
BF16 = jnp.bfloat16

GRID_W = 64
NA_HEADS = 16
NA_WIN_R = 8
NA_WIN_C = 16
NA_QROWS = 8
NA_KROWS = 16
DIFF_HEADS = 8
DIFF_HEAD_DIM = 64
ROPE_THETA = 10000.0
TOP_K = 2
NORM_EPS = 1e-6
HEAD_LANES = 128
MASK_VALUE = -1e30
N_MOD = 6
VMEM_LIMIT = 56 * 1024 * 1024


def _cparams(n_axes, vmem=VMEM_LIMIT):
    return pltpu.CompilerParams(dimension_semantics=("arbitrary",) * n_axes, vmem_limit_bytes=vmem)


def _rms(x):
    return x * lax.rsqrt(jnp.mean(x * x, axis=-1, keepdims=True) + NORM_EPS)


def _ada_kernel(c_ref, w_ref, b_ref, o_ref):
    c = c_ref[...]
    s = c * jax.nn.sigmoid(c)
    o_ref[...] = jnp.dot(s, w_ref[...], preferred_element_type=F32,
                         precision=lax.Precision.HIGHEST) + b_ref[...]


def ada_params(cond, w_ada, b_ada):
    R, D = cond.shape
    N = w_ada.shape[1]
    tn = N // 4
    out = pl.pallas_call(
        _ada_kernel,
        grid=(N // tn,),
        in_specs=[pl.BlockSpec((R, D), lambda j: (0, 0)),
                  pl.BlockSpec((D, tn), lambda j: (0, j)),
                  pl.BlockSpec((1, tn), lambda j: (0, j))],
        out_specs=pl.BlockSpec((R, tn), lambda j: (0, j)),
        out_shape=jax.ShapeDtypeStruct((R, N), F32),
        compiler_params=_cparams(1),
        name="ada_params",
    )(cond, w_ada, b_ada.reshape(1, N))
    return out.reshape(R * N_MOD, 1, D)


def _qkv_kernel(*refs, rope, d_model):
    if rope:
        x_ref, g_ref, sh_ref, sc_ref, w_ref, cos_ref, sa_ref, sb_ref, o_ref = refs
    else:
        x_ref, g_ref, sh_ref, sc_ref, w_ref, o_ref = refs
    h = (_rms(x_ref[...]) * g_ref[...]) * (1.0 + sc_ref[0]) + sh_ref[0]
    hb = h.astype(BF16)
    n_chunks = w_ref.shape[1] // d_model
    for n in range(n_chunks):
        acc = jnp.dot(hb, w_ref[:, n * d_model:(n + 1) * d_model], preferred_element_type=F32)
        if rope and n < 2:
            cos, sa, sb = cos_ref[...], sa_ref[...], sb_ref[...]
            for s in range(d_model // HEAD_LANES):
                xs = acc[:, s * HEAD_LANES:(s + 1) * HEAD_LANES]
                rot = xs * cos + pltpu.roll(xs, HEAD_LANES - 16, 1) * sa + pltpu.roll(xs, 16, 1) * sb
                col = n * d_model + s * HEAD_LANES
                o_ref[:, col:col + HEAD_LANES] = rot.astype(BF16)
        else:
            o_ref[:, n * d_model:(n + 1) * d_model] = acc.astype(BF16)


def norm_mod_qkv(x, gain, mod, mod_base, rows_per_group, w_bf16, tm, rope_tables=None):
    rows, D = x.shape
    N = w_bf16.shape[1]
    tiles_per_group = rows_per_group // tm
    in_specs = [pl.BlockSpec((tm, D), lambda i: (i, 0)),
                pl.BlockSpec((1, D), lambda i: (0, 0)),
                pl.BlockSpec((1, 1, D), lambda i: (mod_base + (i // tiles_per_group) * N_MOD + 0, 0, 0)),
                pl.BlockSpec((1, 1, D), lambda i: (mod_base + (i // tiles_per_group) * N_MOD + 1, 0, 0)),
                pl.BlockSpec((D, N), lambda i: (0, 0))]
    args = [x, gain.reshape(1, D), mod, mod, w_bf16]
    if rope_tables is not None:
        seq_tiles = rope_tables[0].shape[0] // tm
        for t in rope_tables:
            in_specs.append(pl.BlockSpec((tm, HEAD_LANES), lambda i: (i % seq_tiles, 0)))
            args.append(t)
    return pl.pallas_call(
        functools.partial(_qkv_kernel, rope=rope_tables is not None, d_model=D),
        grid=(rows // tm,),
        in_specs=in_specs,
        out_specs=pl.BlockSpec((tm, N), lambda i: (i, 0)),
        out_shape=jax.ShapeDtypeStruct((rows, N), BF16),
        compiler_params=_cparams(1),
        name="norm_mod_qkv",
    )(*args)


def rope_tables(seq):
    half = DIFF_HEAD_DIM // 2
    n_freq = half // 2
    inv_freq = ROPE_THETA ** (-jnp.arange(0, half, 2, dtype=F32) / half)
    pos = jnp.arange(seq)
    rows_pos = (pos // GRID_W).astype(F32)
    cols_pos = (pos % GRID_W).astype(F32)
    lane = jnp.arange(HEAD_LANES)
    d = lane % DIFF_HEAD_DIM
    use_col = (d // half) == 1
    j = d % half
    freq = inv_freq[j % n_freq]
    ang = jnp.where(use_col[None, :], cols_pos[:, None], rows_pos[:, None]) * freq[None, :]
    cos, sin = jnp.cos(ang), jnp.sin(ang)
    first = (j < n_freq)[None, :]
    return cos, jnp.where(first, -sin, 0.0), jnp.where(first, 0.0, sin)


def _head_mask(q, head):
    lane = lax.broadcasted_iota(jnp.int32, (1, HEAD_LANES), 1)
    return jnp.where((lane // 64) == head, q, jnp.zeros_like(q))


def _qk(q, k):
    return lax.dot_general(q, k, (((1,), (1,)), ((), ())), preferred_element_type=F32)


def _softmax2(s_a, s_b):
    m = jnp.maximum(jnp.max(s_a, axis=-1, keepdims=True), jnp.max(s_b, axis=-1, keepdims=True))
    p_a = jnp.exp(s_a - m)
    p_b = jnp.exp(s_b - m)
    denom = jnp.sum(p_a, axis=-1, keepdims=True) + jnp.sum(p_b, axis=-1, keepdims=True)
    return p_a, p_b, 1.0 / denom


def _na_kernel(q_ref, k_ref, v_ref, kc_ref, vc_ref, bias_ref, o_ref, *, grid_rows):
    g = pl.program_id(0)
    k_row0 = jnp.clip(g * NA_QROWS - NA_WIN_R // 2, 0, grid_rows - NA_KROWS) * GRID_W
    k_row0 = pl.multiple_of(k_row0, GRID_W)
    n_keys = NA_KROWS * GRID_W
    q = q_ref[...]
    k_lat = k_ref[pl.ds(k_row0, n_keys), :]
    v_lat = v_ref[pl.ds(k_row0, n_keys), :]
    kc, vc = kc_ref[...], vc_ref[...]
    outs = []
    for head in range(2):
        qm = _head_mask(q, head)
        s_lat = _qk(qm, k_lat) + bias_ref[0, head]
        s_ctx = _qk(qm, kc)
        p_lat, p_ctx, inv = _softmax2(s_lat, s_ctx)
        o = (jnp.dot(p_lat.astype(BF16), v_lat, preferred_element_type=F32)
             + jnp.dot(p_ctx.astype(BF16), vc, preferred_element_type=F32))
        outs.append(o * inv)
    lane = lax.broadcasted_iota(jnp.int32, (1, HEAD_LANES), 1)
    o_ref[...] = jnp.where(lane < 64, outs[0], outs[1]).astype(BF16)


NA_ROW_OFFSETS = 2 * NA_WIN_R - 1
NA_PAIR_ENTRIES = NA_ROW_OFFSETS + 3


def na_pair_table(rpb):
    H = rpb.shape[0]
    qc = jnp.arange(GRID_W)[:, None]
    kcol = jnp.arange(GRID_W)[None, :]
    ws = jnp.clip(qc - NA_WIN_C // 2, 0, GRID_W - NA_WIN_C)
    col_ok = (kcol >= ws) & (kcol < ws + NA_WIN_C)
    dc = jnp.clip(kcol - qc, -(NA_WIN_C - 1), NA_WIN_C - 1) + (NA_WIN_C - 1)
    blocks = jnp.where(col_ok[None, None], rpb[:, :, dc], MASK_VALUE)
    pad = jnp.full((H, 2, GRID_W, GRID_W), MASK_VALUE, F32)
    padded = jnp.concatenate([pad, blocks, pad], axis=1)
    return jnp.concatenate([padded[:, :-1], padded[:, 1:]], axis=-1)


def _na_bias_kernel(tp_ref, o_ref, *, grid_rows):
    g = pl.program_id(0)
    k_row0 = jnp.clip(g * NA_QROWS - NA_WIN_R // 2, 0, grid_rows - NA_KROWS)
    lane = lax.broadcasted_iota(jnp.int32, (1, HEAD_LANES), 1)
    for qr in range(NA_QROWS):
        r = g * NA_QROWS + qr
        rs = jnp.clip(r - NA_WIN_R // 2, 0, grid_rows - NA_WIN_R)
        for pair in range(NA_KROWS // 2):
            kr = k_row0 + 2 * pair
            ok0 = ((kr >= rs) & (kr < rs + NA_WIN_R)).astype(jnp.int32)
            ok1 = ((kr + 1 >= rs) & (kr + 1 < rs + NA_WIN_R)).astype(jnp.int32)
            entry = jnp.clip(kr - r + (NA_WIN_R - 1) + 2, 0, NA_PAIR_ENTRIES - 1)
            ok = jnp.where(lane < GRID_W, ok0, ok1) > 0
            o_ref[0, 0, qr * GRID_W:(qr + 1) * GRID_W, pair * HEAD_LANES:(pair + 1) * HEAD_LANES] = (
                jnp.where(ok, tp_ref[0, entry], MASK_VALUE))


def na_bias(rpb, grid_rows):
    H = rpb.shape[0]
    n_groups = grid_rows // NA_QROWS
    tq, nk = NA_QROWS * GRID_W, NA_KROWS * GRID_W
    return pl.pallas_call(
        functools.partial(_na_bias_kernel, grid_rows=grid_rows),
        grid=(n_groups, H),
        in_specs=[pl.BlockSpec((1, NA_PAIR_ENTRIES, GRID_W, HEAD_LANES), lambda g, h: (h, 0, 0, 0))],
        out_specs=pl.BlockSpec((1, 1, tq, nk), lambda g, h: (g, h, 0, 0)),
        out_shape=jax.ShapeDtypeStruct((n_groups, H, tq, nk), F32),
        compiler_params=_cparams(2),
        name="na_bias",
    )(na_pair_table(rpb))


def neighbourhood_attention(qkv, qkv_ctx, bias, batch, seq, ctx_len, d_model):
    grid_rows = seq // GRID_W
    n_groups = grid_rows // NA_QROWS
    tq = NA_QROWS * GRID_W
    n_hp = d_model // HEAD_LANES
    nk = NA_KROWS * GRID_W
    return pl.pallas_call(
        functools.partial(_na_kernel, grid_rows=grid_rows),
        grid=(n_groups, n_hp, batch),
        in_specs=[pl.BlockSpec((tq, HEAD_LANES), lambda g, h, b: (b * n_groups + g, h)),
                  pl.BlockSpec((seq, HEAD_LANES), lambda g, h, b: (b, n_hp + h)),
                  pl.BlockSpec((seq, HEAD_LANES), lambda g, h, b: (b, 2 * n_hp + h)),
                  pl.BlockSpec((ctx_len, HEAD_LANES), lambda g, h, b: (b, n_hp + h)),
                  pl.BlockSpec((ctx_len, HEAD_LANES), lambda g, h, b: (b, 2 * n_hp + h)),
                  pl.BlockSpec((1, 2, tq, nk), lambda g, h, b: (g, h, 0, 0))],
        out_specs=pl.BlockSpec((tq, HEAD_LANES), lambda g, h, b: (b * n_groups + g, h)),
        out_shape=jax.ShapeDtypeStruct((batch * seq, d_model), BF16),
        compiler_params=_cparams(3),
        name="neighbourhood_attention",
    )(qkv, qkv, qkv, qkv_ctx, qkv_ctx, bias)


def _ctx_attn_kernel(q_ref, k_ref, v_ref, o_ref):
    q, k, v = q_ref[...], k_ref[...], v_ref[...]
    outs = []
    for head in range(2):
        s = _qk(_head_mask(q, head), k)
        m = jnp.max(s, axis=-1, keepdims=True)
        p = jnp.exp(s - m)
        inv = 1.0 / jnp.sum(p, axis=-1, keepdims=True)
        outs.append(jnp.dot(p.astype(BF16), v, preferred_element_type=F32) * inv)
    lane = lax.broadcasted_iota(jnp.int32, (1, HEAD_LANES), 1)
    o_ref[...] = jnp.where(lane < 64, outs[0], outs[1]).astype(BF16)


def ctx_attention(qkv_ctx, batch, ctx_len, d_model):
    n_hp = d_model // HEAD_LANES
    return pl.pallas_call(
        _ctx_attn_kernel,
        grid=(batch, n_hp),
        in_specs=[pl.BlockSpec((ctx_len, HEAD_LANES), lambda b, h: (b, h)),
                  pl.BlockSpec((ctx_len, HEAD_LANES), lambda b, h: (b, n_hp + h)),
                  pl.BlockSpec((ctx_len, HEAD_LANES), lambda b, h: (b, 2 * n_hp + h))],
        out_specs=pl.BlockSpec((ctx_len, HEAD_LANES), lambda b, h: (b, h)),
        out_shape=jax.ShapeDtypeStruct((batch * ctx_len, d_model), BF16),
        compiler_params=_cparams(2),
        name="ctx_attention",
    )(qkv_ctx, qkv_ctx, qkv_ctx)


def _diff_attn_kernel(q_ref, k_ref, v_ref, kc_ref, vc_ref, lam_ref, subln_ref, o_ref, *, lam_init, sub_rows):
    lp = lam_ref[...]
    lam = (jnp.exp(jnp.sum(lp[0:1] * lp[1:2], axis=-1, keepdims=True))
           - jnp.exp(jnp.sum(lp[2:3] * lp[3:4], axis=-1, keepdims=True)) + lam_init)
    k, v, kc, vc = k_ref[...], v_ref[...], kc_ref[...], vc_ref[...]

    def attend(qm):
        p_lat, p_ctx, inv = _softmax2(_qk(qm, k), _qk(qm, kc))
        return (jnp.dot(p_lat.astype(BF16), v, preferred_element_type=F32)
                + jnp.dot(p_ctx.astype(BF16), vc, preferred_element_type=F32)), inv

    for r0 in range(0, q_ref.shape[0], sub_rows):
        q = q_ref[r0:r0 + sub_rows, :]
        o1, inv1 = attend(_head_mask(q, 0))
        o2, inv2 = attend(_head_mask(q, 1))
        o = o1 * inv1 - o2 * (lam * inv2)
        o_ref[r0:r0 + sub_rows, :] = ((_rms(o) * subln_ref[...]) * (1.0 - lam_init)).astype(BF16)


def diff_attention(qkv, qkv_ctx, lam_params, subln, lam_init, batch, seq, ctx_len, d_model, tq, sub_rows=256):
    n_h = d_model // HEAD_LANES
    nq = seq // tq
    return pl.pallas_call(
        functools.partial(_diff_attn_kernel, lam_init=lam_init, sub_rows=min(sub_rows, tq)),
        grid=(batch, n_h, nq),
        in_specs=[pl.BlockSpec((tq, HEAD_LANES), lambda b, h, i: (b * nq + i, h)),
                  pl.BlockSpec((seq, HEAD_LANES), lambda b, h, i: (b, n_h + h)),
                  pl.BlockSpec((seq, HEAD_LANES), lambda b, h, i: (b, 2 * n_h + h)),
                  pl.BlockSpec((ctx_len, HEAD_LANES), lambda b, h, i: (b, n_h + h)),
                  pl.BlockSpec((ctx_len, HEAD_LANES), lambda b, h, i: (b, 2 * n_h + h)),
                  pl.BlockSpec(lam_params.shape, lambda b, h, i: (0, 0)),
                  pl.BlockSpec((1, HEAD_LANES), lambda b, h, i: (0, 0))],
        out_specs=pl.BlockSpec((tq, HEAD_LANES), lambda b, h, i: (b * nq + i, h)),
        out_shape=jax.ShapeDtypeStruct((batch * seq, d_model), BF16),
        compiler_params=_cparams(3),
        name="diff_attention",
    )(qkv, qkv, qkv, qkv_ctx, qkv_ctx, lam_params, subln.reshape(1, HEAD_LANES))


def _oproj_kernel(*refs, route, n_experts):
    if route:
        o_ref, w_ref, x_ref, g1_ref, gain_ref, sh_ref, sc_ref, wr_ref, xo_ref, h_ref, r_ref = refs
    else:
        o_ref, w_ref, x_ref, g1_ref, gain_ref, sh_ref, sc_ref, xo_ref, h_ref = refs
    y = jnp.dot(o_ref[...], w_ref[...], preferred_element_type=F32)
    xn = x_ref[...] + g1_ref[0] * y
    xo_ref[...] = xn
    h = (_rms(xn) * gain_ref[...]) * (1.0 + sc_ref[0]) + sh_ref[0]
    if not route:
        h_ref[...] = h.astype(BF16)
        return
    h_ref[...] = h
    n_exp = n_experts
    h_hi = h.astype(BF16)
    h_lo = (h - h_hi.astype(F32)).astype(BF16)
    both = jnp.dot(h_hi, wr_ref[...], preferred_element_type=F32)
    logits = both[:, :HEAD_LANES] + (jnp.dot(h_lo, wr_ref[:, :HEAD_LANES], preferred_element_type=F32)
                                     + both[:, HEAD_LANES:])
    lane = lax.broadcasted_iota(jnp.int32, logits.shape, 1)
    logits = jnp.where(lane < n_exp, logits, -jnp.inf)
    v1 = jnp.max(logits, axis=-1, keepdims=True)
    i1 = jnp.min(jnp.where(logits == v1, lane, n_exp), axis=-1, keepdims=True)
    rest = jnp.where(lane == i1, -jnp.inf, logits)
    v2 = jnp.max(rest, axis=-1, keepdims=True)
    i2 = jnp.min(jnp.where(rest == v2, lane, n_exp), axis=-1, keepdims=True)
    e = jnp.exp(v2 - v1)
    w1 = 1.0 / (1.0 + e)
    w2 = e * w1
    out_lane = lax.broadcasted_iota(jnp.int32, r_ref.shape, 1)
    r_ref[...] = jnp.where(out_lane == 0, i1.astype(F32),
                           jnp.where(out_lane == 1, i2.astype(F32),
                                     jnp.where(out_lane == 2, w1, jnp.where(out_lane == 3, w2, 0.0))))


def oproj_residual_norm(o, w_o_bf16, x, gain, mod, mod_base, rows_per_group, tm, w_router=None, n_experts=0):
    rows, D = x.shape
    tiles_per_group = rows_per_group // tm

    def mod_spec(k):
        return pl.BlockSpec((1, 1, D), lambda i: (mod_base + (i // tiles_per_group) * N_MOD + k, 0, 0))

    in_specs = [pl.BlockSpec((tm, D), lambda i: (i, 0)),
                pl.BlockSpec((D, D), lambda i: (0, 0)),
                pl.BlockSpec((tm, D), lambda i: (i, 0)),
                mod_spec(2),
                pl.BlockSpec((1, D), lambda i: (0, 0)),
                mod_spec(3), mod_spec(4)]
    args = [o, w_o_bf16, x, mod, gain.reshape(1, D), mod, mod]
    out_specs = [pl.BlockSpec((tm, D), lambda i: (i, 0)), pl.BlockSpec((tm, D), lambda i: (i, 0))]
    out_shape = [jax.ShapeDtypeStruct((rows, D), F32),
                 jax.ShapeDtypeStruct((rows, D), BF16 if w_router is None else F32)]
    if w_router is not None:
        in_specs.append(pl.BlockSpec(w_router.shape, lambda i: (0, 0)))
        args.append(w_router)
        out_specs.append(pl.BlockSpec((tm, HEAD_LANES), lambda i: (i, 0)))
        out_shape.append(jax.ShapeDtypeStruct((rows, HEAD_LANES), F32))
    return pl.pallas_call(
        functools.partial(_oproj_kernel, route=w_router is not None, n_experts=n_experts),
        grid=(rows // tm,),
        in_specs=in_specs,
        out_specs=out_specs,
        out_shape=out_shape,
        compiler_params=_cparams(1),
        name="oproj_residual_norm",
    )(*args)


def _swiglu_chunk(h, wg, wu, wd):
    a = jnp.dot(h, wg, preferred_element_type=F32)
    b = jnp.dot(h, wu, preferred_element_type=F32)
    t = (a * jax.nn.sigmoid(a)) * b
    return jnp.dot(t.astype(BF16), wd, preferred_element_type=F32)


def _ffn_kernel(h_ref, wg_ref, wu_ref, wd_ref, x_ref, g2_ref, o_ref, acc_ref):
    j = pl.program_id(1)

    @pl.when(j == 0)
    def _():
        acc_ref[...] = jnp.zeros_like(acc_ref)

    acc_ref[...] += _swiglu_chunk(h_ref[...], wg_ref[...], wu_ref[...], wd_ref[...])

    @pl.when(j == pl.num_programs(1) - 1)
    def _():
        o_ref[...] = x_ref[...] + g2_ref[0] * acc_ref[...]


def ffn_residual(h, wg, wu, wd, x, mod, mod_base, rows_per_group, tm, tf):
    rows, D = x.shape
    F = wg.shape[1]
    tiles_per_group = rows_per_group // tm
    return pl.pallas_call(
        _ffn_kernel,
        grid=(rows // tm, F // tf),
        in_specs=[pl.BlockSpec((tm, D), lambda i, j: (i, 0)),
                  pl.BlockSpec((D, tf), lambda i, j: (0, j)),
                  pl.BlockSpec((D, tf), lambda i, j: (0, j)),
                  pl.BlockSpec((tf, D), lambda i, j: (j, 0)),
                  pl.BlockSpec((tm, D), lambda i, j: (i, 0)),
                  pl.BlockSpec((1, 1, D), lambda i, j: (mod_base + (i // tiles_per_group) * N_MOD + 5, 0, 0))],
        out_specs=pl.BlockSpec((tm, D), lambda i, j: (i, 0)),
        out_shape=jax.ShapeDtypeStruct((rows, D), F32),
        scratch_shapes=[pltpu.VMEM((tm, D), F32)],
        compiler_params=_cparams(2),
        name="ffn_residual",
    )(h, wg, wu, wd, x, mod)


def _gather_rows_kernel(idx_ref, src_ref, o_ref, sem):
    n = o_ref.shape[0]

    def row_copy(r):
        return pltpu.make_async_copy(src_ref.at[pl.ds(idx_ref[r], 1)], o_ref.at[pl.ds(r, 1)], sem)

    def start(r, c):
        row_copy(r).start()
        return c

    def wait(r, c):
        row_copy(r).wait()
        return c

    lax.fori_loop(0, n, start, 0)
    lax.fori_loop(0, n, wait, 0)


def gather_rows(src, idx, tg):
    n = idx.shape[0]
    D = src.shape[1]
    return pl.pallas_call(
        _gather_rows_kernel,
        grid=(n // tg,),
        in_specs=[pl.BlockSpec((tg,), lambda i: (i,), memory_space=pltpu.SMEM),
                  pl.BlockSpec(memory_space=pl.ANY)],
        out_specs=pl.BlockSpec((tg, D), lambda i: (i, 0)),
        out_shape=jax.ShapeDtypeStruct((n, D), src.dtype),
        scratch_shapes=[pltpu.SemaphoreType.DMA(())],
        compiler_params=_cparams(1),
        name="gather_rows",
    )(idx, src)


def _expert_ffn_kernel(te_ref, valid_ref, h_ref, wg_ref, wu_ref, wd_ref, o_ref, hb_ref, acc_ref):
    i, j = pl.program_id(0), pl.program_id(1)
    last = pl.num_programs(1) - 1

    @pl.when(j == 0)
    def _():
        acc_ref[...] = jnp.zeros_like(acc_ref)
        hb_ref[...] = h_ref[...].astype(BF16)

    @pl.when(valid_ref[i] > 0)
    def _():
        acc_ref[...] += _swiglu_chunk(hb_ref[...], wg_ref[...], wu_ref[...], wd_ref[...])

    @pl.when(j == last)
    def _():
        o_ref[...] = acc_ref[...]


def expert_ffn(h_sorted, tile_expert, tile_valid, wg, wu, wd, tm, tf):
    rows, D = h_sorted.shape
    F = wg.shape[2]
    grid_spec = pltpu.PrefetchScalarGridSpec(
        num_scalar_prefetch=2,
        grid=(rows // tm, F // tf),
        in_specs=[pl.BlockSpec((tm, D), lambda i, j, te, va: (i, 0)),
                  pl.BlockSpec((None, D, tf), lambda i, j, te, va: (te[i], 0, j)),
                  pl.BlockSpec((None, D, tf), lambda i, j, te, va: (te[i], 0, j)),
                  pl.BlockSpec((None, tf, D), lambda i, j, te, va: (te[i], j, 0))],
        out_specs=pl.BlockSpec((tm, D), lambda i, j, te, va: (i, 0)),
        scratch_shapes=[pltpu.VMEM((tm, D), BF16), pltpu.VMEM((tm, D), F32)],
    )
    return pl.pallas_call(
        _expert_ffn_kernel,
        grid_spec=grid_spec,
        out_shape=jax.ShapeDtypeStruct((rows, D), F32),
        compiler_params=_cparams(2),
        name="expert_ffn",
    )(tile_expert, tile_valid, h_sorted, wg, wu, wd)


def _combine_kernel(p1_ref, p2_ref, y_ref, route_ref, x_ref, g2_ref, gain_ref, o_ref, buf_ref, sem):
    n = o_ref.shape[0]

    def row_copy(slot, idx_ref, r):
        return pltpu.make_async_copy(y_ref.at[pl.ds(idx_ref[r], 1)], buf_ref.at[slot, pl.ds(r, 1)], sem)

    def start(r, c):
        row_copy(0, p1_ref, r).start()
        row_copy(1, p2_ref, r).start()
        return c

    def wait(r, c):
        row_copy(0, p1_ref, r).wait()
        row_copy(1, p2_ref, r).wait()
        return c

    lax.fori_loop(0, n, start, 0)
    lax.fori_loop(0, n, wait, 0)
    route = route_ref[...]
    y = route[:, 2:3] * buf_ref[0] + route[:, 3:4] * buf_ref[1]
    xn = x_ref[...] + g2_ref[0] * y
    o_ref[...] = _rms(xn) * gain_ref[...]


def combine_final(y_sorted, pos1, pos2, route, x, mod, mod_base, rows_per_group, final_gain, tm):
    rows, D = x.shape
    tiles_per_group = rows_per_group // tm
    return pl.pallas_call(
        _combine_kernel,
        grid=(rows // tm,),
        in_specs=[pl.BlockSpec((tm,), lambda i: (i,), memory_space=pltpu.SMEM),
                  pl.BlockSpec((tm,), lambda i: (i,), memory_space=pltpu.SMEM),
                  pl.BlockSpec(memory_space=pl.ANY),
                  pl.BlockSpec((tm, HEAD_LANES), lambda i: (i, 0)),
                  pl.BlockSpec((tm, D), lambda i: (i, 0)),
                  pl.BlockSpec((1, 1, D), lambda i: (mod_base + (i // tiles_per_group) * N_MOD + 5, 0, 0)),
                  pl.BlockSpec((1, D), lambda i: (0, 0))],
        out_specs=pl.BlockSpec((tm, D), lambda i: (i, 0)),
        out_shape=jax.ShapeDtypeStruct((rows, D), F32),
        scratch_shapes=[pltpu.VMEM((2, tm, D), F32), pltpu.SemaphoreType.DMA(())],
        compiler_params=_cparams(1),
        name="combine_final",
    )(pos1, pos2, y_sorted, route, x, mod, final_gain.reshape(1, D))


def routing_plan(route, n_experts, tm):
    T = route.shape[0]
    e = jnp.concatenate([route[:, 0], route[:, 1]]).astype(jnp.int32)
    onehot = (e[:, None] == jnp.arange(n_experts)[None, :]).astype(jnp.int32)
    csum = jnp.cumsum(onehot, axis=0)
    rank = jnp.sum((csum - onehot) * onehot, axis=1)
    count = csum[-1]
    padded = ((count + tm - 1) // tm) * tm
    ends = jnp.cumsum(padded)
    offset = ends - padded
    pos = jnp.sum(onehot * offset[None, :], axis=1) + rank
    n_tiles = (TOP_K * T) // tm + n_experts
    tile_start = jnp.arange(n_tiles) * tm
    tile_valid = (tile_start < ends[-1]).astype(jnp.int32)
    tile_expert = jnp.sum((tile_start[:, None] >= ends[None, :]).astype(jnp.int32), axis=1)
    last_expert = jnp.max(jnp.where(tile_valid > 0, tile_expert, 0))
    tile_expert = jnp.where(tile_valid > 0, tile_expert, last_expert)
    src = jnp.zeros((n_tiles * tm,), jnp.int32).at[pos].set(jnp.arange(TOP_K * T, dtype=jnp.int32) % T)
    return src, pos[:T], pos[T:], tile_expert, tile_valid


def kernel(x, c, ctx, c_ctx, l0_w_ada, l0_b_ada, l0_norm_mix, l0_w_qkv, l0_rpb, l0_w_o, l0_norm_ffn, l0_w_gate, l0_w_up, l0_w_down, l1_w_ada, l1_b_ada, l1_norm_mix, l1_w_qkv, l1_lambda_q1, l1_lambda_k1, l1_lambda_q2, l1_lambda_k2, l1_subln, l1_w_o, l1_norm_ffn, l1_w_router, l1_w_gate, l1_w_up, l1_w_down, final_norm):
    B, S, D = x.shape
    C = ctx.shape[1]
    T, TC = B * S, B * C
    n_experts = l1_w_router.shape[1]
    tm = min(512, S)
    tmc = min(512, TC)

    xf = x.reshape(T, D)
    cf = ctx.reshape(TC, D)

    n_cond = ((B + 1 + 7) // 8) * 8
    cond = jnp.zeros((n_cond, D), F32).at[:B].set(c).at[B].set(c_ctx)
    ctx_base = B * N_MOD

    def qkv_weight(w, scale):
        col_scale = jnp.concatenate([jnp.full((D,), scale, F32), jnp.ones((2 * D,), F32)])
        return (w * col_scale[None, :]).astype(BF16)

    mod0 = ada_params(cond, l0_w_ada, l0_b_ada)
    w_qkv0 = qkv_weight(l0_w_qkv, (D // NA_HEADS) ** -0.5)
    qkv = norm_mod_qkv(xf, l0_norm_mix, mod0, 0, S, w_qkv0, tm)
    qkv_c = norm_mod_qkv(cf, l0_norm_mix, mod0, ctx_base, TC, w_qkv0, tmc)
    bias = na_bias(l0_rpb, S // GRID_W)
    o = neighbourhood_attention(qkv, qkv_c, bias, B, S, C, D)
    oc = ctx_attention(qkv_c, B, C, D)
    w_o0 = l0_w_o.astype(BF16)
    xf, h = oproj_residual_norm(o, w_o0, xf, l0_norm_ffn, mod0, 0, S, tm)
    cf, hc = oproj_residual_norm(oc, w_o0, cf, l0_norm_ffn, mod0, ctx_base, TC, tmc)
    wg0, wu0, wd0 = l0_w_gate.astype(BF16), l0_w_up.astype(BF16), l0_w_down.astype(BF16)
    d_ff = l0_w_gate.shape[1]
    tf0 = d_ff // 2 if (d_ff // 2) % 128 == 0 else d_ff
    xf = ffn_residual(h, wg0, wu0, wd0, xf, mod0, 0, S, tm, tf0)
    cf = ffn_residual(hc, wg0, wu0, wd0, cf, mod0, ctx_base, TC, tmc, tf0)

    mod1 = ada_params(cond, l1_w_ada, l1_b_ada)
    w_qkv1 = qkv_weight(l1_w_qkv, DIFF_HEAD_DIM ** -0.5)
    qkv = norm_mod_qkv(xf, l1_norm_mix, mod1, 0, S, w_qkv1, tm, rope_tables=rope_tables(S))
    qkv_c = norm_mod_qkv(cf, l1_norm_mix, mod1, ctx_base, TC, w_qkv1, tmc)
    lam_init = 0.8 - 0.6 * math.exp(-0.3 * 1)
    lam_params = jnp.stack([l1_lambda_q1, l1_lambda_k1, l1_lambda_q2, l1_lambda_k2]).astype(F32)
    o = diff_attention(qkv, qkv_c, lam_params, l1_subln, lam_init, B, S, C, D, tq=min(512, S))
    w_router = jnp.zeros((D, HEAD_LANES), F32).at[:, :n_experts].set(l1_w_router)
    w_router_hi = w_router.astype(BF16)
    w_router = jnp.concatenate([w_router_hi, (w_router - w_router_hi.astype(F32)).astype(BF16)], axis=1)
    xf, h, route = oproj_residual_norm(o, l1_w_o.astype(BF16), xf, l1_norm_ffn, mod1, 0, S, tm,
                                       w_router=w_router, n_experts=n_experts)
    tme = min(512, S)
    src, pos1, pos2, tile_expert, tile_valid = routing_plan(route, n_experts, tme)
    h_sorted = gather_rows(h, src, tme)
    d_ffe = l1_w_gate.shape[2]
    tfe = d_ffe // 2 if (d_ffe // 2) % 128 == 0 else d_ffe
    y_sorted = expert_ffn(h_sorted, tile_expert, tile_valid, l1_w_gate.astype(BF16), l1_w_up.astype(BF16),
                          l1_w_down.astype(BF16), tme, tfe)
    out = combine_final(y_sorted, pos1, pos2, route, xf, mod1, 0, S, final_norm, min(256, S))
    return out.reshape(B, S, D)
```

```python
import functools
import math

import jax
import jax.numpy as jnp
from jax import lax
from jax.experimental import pallas as pl
from jax.experimental.pallas import tpu as pltpu

F32 = jnp.float32
BF16 = jnp.bfloat16

GRID_W = 64
NA_HEADS = 16
NA_WIN_R = 8
NA_WIN_C = 16
NA_QROWS = 8
NA_KROWS = 16
DIFF_HEADS = 8
DIFF_HEAD_DIM = 64
ROPE_THETA = 10000.0
TOP_K = 2
NORM_EPS = 1e-6
HEAD_LANES = 128
MASK_VALUE = -1e30
N_MOD = 6
VMEM_LIMIT = 56 * 1024 * 1024


def _cparams(n_axes, vmem=VMEM_LIMIT):
    return pltpu.CompilerParams(dimension_semantics=("arbitrary",) * n_axes, vmem_limit_bytes=vmem)


def _rms(x):
    return x * lax.rsqrt(jnp.mean(x * x, axis=-1, keepdims=True) + NORM_EPS)


def _ada_kernel(c_ref, w_ref, b_ref, o_ref):
    c = c_ref[...]
    s = c * jax.nn.sigmoid(c)
    o_ref[...] = jnp.dot(s, w_ref[...], preferred_element_type=F32,
                         precision=lax.Precision.HIGHEST) + b_ref[...]


def ada_params(cond, w_ada, b_ada):
    R, D = cond.shape
    N = w_ada.shape[1]
    tn = N // 4
    out = pl.pallas_call(
        _ada_kernel,
        grid=(N // tn,),
        in_specs=[pl.BlockSpec((R, D), lambda j: (0, 0)),
                  pl.BlockSpec((D, tn), lambda j: (0, j)),
                  pl.BlockSpec((1, tn), lambda j: (0, j))],
        out_specs=pl.BlockSpec((R, tn), lambda j: (0, j)),
        out_shape=jax.ShapeDtypeStruct((R, N), F32),
        compiler_params=_cparams(1),
        name="ada_params",
    )(cond, w_ada, b_ada.reshape(1, N))
    return out.reshape(R * N_MOD, 1, D)


def _qkv_kernel(*refs, rope, d_model):
    if rope:
        x_ref, g_ref, sh_ref, sc_ref, w_ref, cos_ref, sa_ref, sb_ref, o_ref = refs
    else:
        x_ref, g_ref, sh_ref, sc_ref, w_ref, o_ref = refs
    h = (_rms(x_ref[...]) * g_ref[...]) * (1.0 + sc_ref[0]) + sh_ref[0]
    hb = h.astype(BF16)
    n_chunks = w_ref.shape[1] // d_model
    for n in range(n_chunks):
        acc = jnp.dot(hb, w_ref[:, n * d_model:(n + 1) * d_model], preferred_element_type=F32)
        if rope and n < 2:
            cos, sa, sb = cos_ref[...], sa_ref[...], sb_ref[...]
            for s in range(d_model // HEAD_LANES):
                xs = acc[:, s * HEAD_LANES:(s + 1) * HEAD_LANES]
                rot = xs * cos + pltpu.roll(xs, HEAD_LANES - 16, 1) * sa + pltpu.roll(xs, 16, 1) * sb
                col = n * d_model + s * HEAD_LANES
                o_ref[:, col:col + HEAD_LANES] = rot.astype(BF16)
        else:
            o_ref[:, n * d_model:(n + 1) * d_model] = acc.astype(BF16)


def norm_mod_qkv(x, gain, mod, mod_base, rows_per_group, w_bf16, tm, rope_tables=None):
    rows, D = x.shape
    N = w_bf16.shape[1]
    tiles_per_group = rows_per_group // tm
    in_specs = [pl.BlockSpec((tm, D), lambda i: (i, 0)),
                pl.BlockSpec((1, D), lambda i: (0, 0)),
                pl.BlockSpec((1, 1, D), lambda i: (mod_base + (i // tiles_per_group) * N_MOD + 0, 0, 0)),
                pl.BlockSpec((1, 1, D), lambda i: (mod_base + (i // tiles_per_group) * N_MOD + 1, 0, 0)),
                pl.BlockSpec((D, N), lambda i: (0, 0))]
    args = [x, gain.reshape(1, D), mod, mod, w_bf16]
    if rope_tables is not None:
        seq_tiles = rope_tables[0].shape[0] // tm
        for t in rope_tables:
            in_specs.append(pl.BlockSpec((tm, HEAD_LANES), lambda i: (i % seq_tiles, 0)))
            args.append(t)
    return pl.pallas_call(
        functools.partial(_qkv_kernel, rope=rope_tables is not None, d_model=D),
        grid=(rows // tm,),
        in_specs=in_specs,
        out_specs=pl.BlockSpec((tm, N), lambda i: (i, 0)),
        out_shape=jax.ShapeDtypeStruct((rows, N), BF16),
        compiler_params=_cparams(1),
        name="norm_mod_qkv",
    )(*args)


def rope_tables(seq):
    half = DIFF_HEAD_DIM // 2
    n_freq = half // 2
    inv_freq = ROPE_THETA ** (-jnp.arange(0, half, 2, dtype=F32) / half)
    pos = jnp.arange(seq)
    rows_pos = (pos // GRID_W).astype(F32)
    cols_pos = (pos % GRID_W).astype(F32)
    lane = jnp.arange(HEAD_LANES)
    d = lane % DIFF_HEAD_DIM
    use_col = (d // half) == 1
    j = d % half
    freq = inv_freq[j % n_freq]
    ang = jnp.where(use_col[None, :], cols_pos[:, None], rows_pos[:, None]) * freq[None, :]
    cos, sin = jnp.cos(ang), jnp.sin(ang)
    first = (j < n_freq)[None, :]
    return cos, jnp.where(first, -sin, 0.0), jnp.where(first, 0.0, sin)


def _head_mask(q, head):
    lane = lax.broadcasted_iota(jnp.int32, (1, HEAD_LANES), 1)
    return jnp.where((lane // 64) == head, q, jnp.zeros_like(q))


def _qk(q, k):
    return lax.dot_general(q, k, (((1,), (1,)), ((), ())), preferred_element_type=F32)


def _softmax2(s_a, s_b):
    m = jnp.maximum(jnp.max(s_a, axis=-1, keepdims=True), jnp.max(s_b, axis=-1, keepdims=True))
    p_a = jnp.exp(s_a - m)
    p_b = jnp.exp(s_b - m)
    denom = jnp.sum(p_a, axis=-1, keepdims=True) + jnp.sum(p_b, axis=-1, keepdims=True)
    return p_a, p_b, 1.0 / denom


def _na_kernel(q_ref, k_ref, v_ref, kc_ref, vc_ref, bias_ref, o_ref, *, grid_rows):
    g = pl.program_id(0)
    k_row0 = jnp.clip(g * NA_QROWS - NA_WIN_R // 2, 0, grid_rows - NA_KROWS) * GRID_W
    k_row0 = pl.multiple_of(k_row0, GRID_W)
    n_keys = NA_KROWS * GRID_W
    q = q_ref[...]
    k_lat = k_ref[pl.ds(k_row0, n_keys), :]
    v_lat = v_ref[pl.ds(k_row0, n_keys), :]
    kc, vc = kc_ref[...], vc_ref[...]
    outs = []
    for head in range(2):
        qm = _head_mask(q, head)
        s_lat = _qk(qm, k_lat) + bias_ref[0, head]
        s_ctx = _qk(qm, kc)
        p_lat, p_ctx, inv = _softmax2(s_lat, s_ctx)
        o = (jnp.dot(p_lat.astype(BF16), v_lat, preferred_element_type=F32)
             + jnp.dot(p_ctx.astype(BF16), vc, preferred_element_type=F32))
        outs.append(o * inv)
    lane = lax.broadcasted_iota(jnp.int32, (1, HEAD_LANES), 1)
    o_ref[...] = jnp.where(lane < 64, outs[0], outs[1]).astype(BF16)


NA_ROW_OFFSETS = 2 * NA_WIN_R - 1
NA_PAIR_ENTRIES = NA_ROW_OFFSETS + 3


def na_pair_table(rpb):
    H = rpb.shape[0]
    qc = jnp.arange(GRID_W)[:, None]
    kcol = jnp.arange(GRID_W)[None, :]
    ws = jnp.clip(qc - NA_WIN_C // 2, 0, GRID_W - NA_WIN_C)
    col_ok = (kcol >= ws) & (kcol < ws + NA_WIN_C)
    dc = jnp.clip(kcol - qc, -(NA_WIN_C - 1), NA_WIN_C - 1) + (NA_WIN_C - 1)
    blocks = jnp.where(col_ok[None, None], rpb[:, :, dc], MASK_VALUE)
    pad = jnp.full((H, 2, GRID_W, GRID_W), MASK_VALUE, F32)
    padded = jnp.concatenate([pad, blocks, pad], axis=1)
    return jnp.concatenate([padded[:, :-1], padded[:, 1:]], axis=-1)


def _na_bias_kernel(tp_ref, o_ref, *, grid_rows):
    g = pl.program_id(0)
    k_row0 = jnp.clip(g * NA_QROWS - NA_WIN_R // 2, 0, grid_rows - NA_KROWS)
    lane = lax.broadcasted_iota(jnp.int32, (1, HEAD_LANES), 1)
    for qr in range(NA_QROWS):
        r = g * NA_QROWS + qr
        rs = jnp.clip(r - NA_WIN_R // 2, 0, grid_rows - NA_WIN_R)
        for pair in range(NA_KROWS // 2):
            kr = k_row0 + 2 * pair
            ok0 = ((kr >= rs) & (kr < rs + NA_WIN_R)).astype(jnp.int32)
            ok1 = ((kr + 1 >= rs) & (kr + 1 < rs + NA_WIN_R)).astype(jnp.int32)
            entry = jnp.clip(kr - r + (NA_WIN_R - 1) + 2, 0, NA_PAIR_ENTRIES - 1)
            ok = jnp.where(lane < GRID_W, ok0, ok1) > 0
            o_ref[0, 0, qr * GRID_W:(qr + 1) * GRID_W, pair * HEAD_LANES:(pair + 1) * HEAD_LANES] = (
                jnp.where(ok, tp_ref[0, entry], MASK_VALUE))


def na_bias(rpb, grid_rows):
    H = rpb.shape[0]
    n_groups = grid_rows // NA_QROWS
    tq, nk = NA_QROWS * GRID_W, NA_KROWS * GRID_W
    return pl.pallas_call(
        functools.partial(_na_bias_kernel, grid_rows=grid_rows),
        grid=(n_groups, H),
        in_specs=[pl.BlockSpec((1, NA_PAIR_ENTRIES, GRID_W, HEAD_LANES), lambda g, h: (h, 0, 0, 0))],
        out_specs=pl.BlockSpec((1, 1, tq, nk), lambda g, h: (g, h, 0, 0)),
        out_shape=jax.ShapeDtypeStruct((n_groups, H, tq, nk), F32),
        compiler_params=_cparams(2),
        name="na_bias",
    )(na_pair_table(rpb))


def neighbourhood_attention(qkv, qkv_ctx, bias, batch, seq, ctx_len, d_model):
    grid_rows = seq // GRID_W
    n_groups = grid_rows // NA_QROWS
    tq = NA_QROWS * GRID_W
    n_hp = d_model // HEAD_LANES
    nk = NA_KROWS * GRID_W
    return pl.pallas_call(
        functools.partial(_na_kernel, grid_rows=grid_rows),
        grid=(n_groups, n_hp, batch),
        in_specs=[pl.BlockSpec((tq, HEAD_LANES), lambda g, h, b: (b * n_groups + g, h)),
                  pl.BlockSpec((seq, HEAD_LANES), lambda g, h, b: (b, n_hp + h)),
                  pl.BlockSpec((seq, HEAD_LANES), lambda g, h, b: (b, 2 * n_hp + h)),
                  pl.BlockSpec((ctx_len, HEAD_LANES), lambda g, h, b: (b, n_hp + h)),
                  pl.BlockSpec((ctx_len, HEAD_LANES), lambda g, h, b: (b, 2 * n_hp + h)),
                  pl.BlockSpec((1, 2, tq, nk), lambda g, h, b: (g, h, 0, 0))],
        out_specs=pl.BlockSpec((tq, HEAD_LANES), lambda g, h, b: (b * n_groups + g, h)),
        out_shape=jax.ShapeDtypeStruct((batch * seq, d_model), BF16),
        compiler_params=_cparams(3),
        name="neighbourhood_attention",
    )(qkv, qkv, qkv, qkv_ctx, qkv_ctx, bias)


def _ctx_attn_kernel(q_ref, k_ref, v_ref, o_ref):
    q, k, v = q_ref[...], k_ref[...], v_ref[...]
    outs = []
    for head in range(2):
        s = _qk(_head_mask(q, head), k)
        m = jnp.max(s, axis=-1, keepdims=True)
        p = jnp.exp(s - m)
        inv = 1.0 / jnp.sum(p, axis=-1, keepdims=True)
        outs.append(jnp.dot(p.astype(BF16), v, preferred_element_type=F32) * inv)
    lane = lax.broadcasted_iota(jnp.int32, (1, HEAD_LANES), 1)
    o_ref[...] = jnp.where(lane < 64, outs[0], outs[1]).astype(BF16)


def ctx_attention(qkv_ctx, batch, ctx_len, d_model):
    n_hp = d_model // HEAD_LANES
    return pl.pallas_call(
        _ctx_attn_kernel,
        grid=(batch, n_hp),
        in_specs=[pl.BlockSpec((ctx_len, HEAD_LANES), lambda b, h: (b, h)),
                  pl.BlockSpec((ctx_len, HEAD_LANES), lambda b, h: (b, n_hp + h)),
                  pl.BlockSpec((ctx_len, HEAD_LANES), lambda b, h: (b, 2 * n_hp + h))],
        out_specs=pl.BlockSpec((ctx_len, HEAD_LANES), lambda b, h: (b, h)),
        out_shape=jax.ShapeDtypeStruct((batch * ctx_len, d_model), BF16),
        compiler_params=_cparams(2),
        name="ctx_attention",
    )(qkv_ctx, qkv_ctx, qkv_ctx)


def _diff_attn_kernel(q_ref, k_ref, v_ref, kc_ref, vc_ref, lam_ref, subln_ref, o_ref, *, lam_init, sub_rows):
    lp = lam_ref[...]
    lam = (jnp.exp(jnp.sum(lp[0:1] * lp[1:2], axis=-1, keepdims=True))
           - jnp.exp(jnp.sum(lp[2:3] * lp[3:4], axis=-1, keepdims=True)) + lam_init)
    k, v, kc, vc = k_ref[...], v_ref[...], kc_ref[...], vc_ref[...]

    def attend(qm):
        p_lat, p_ctx, inv = _softmax2(_qk(qm, k), _qk(qm, kc))
        return (jnp.dot(p_lat.astype(BF16), v, preferred_element_type=F32)
                + jnp.dot(p_ctx.astype(BF16), vc, preferred_element_type=F32)), inv

    for r0 in range(0, q_ref.shape[0], sub_rows):
        q = q_ref[r0:r0 + sub_rows, :]
        o1, inv1 = attend(_head_mask(q, 0))
        o2, inv2 = attend(_head_mask(q, 1))
        o = o1 * inv1 - o2 * (lam * inv2)
        o_ref[r0:r0 + sub_rows, :] = ((_rms(o) * subln_ref[...]) * (1.0 - lam_init)).astype(BF16)


def diff_attention(qkv, qkv_ctx, lam_params, subln, lam_init, batch, seq, ctx_len, d_model, tq, sub_rows=256):
    n_h = d_model // HEAD_LANES
    nq = seq // tq
    return pl.pallas_call(
        functools.partial(_diff_attn_kernel, lam_init=lam_init, sub_rows=min(sub_rows, tq)),
        grid=(batch, n_h, nq),
        in_specs=[pl.BlockSpec((tq, HEAD_LANES), lambda b, h, i: (b * nq + i, h)),
                  pl.BlockSpec((seq, HEAD_LANES), lambda b, h, i: (b, n_h + h)),
                  pl.BlockSpec((seq, HEAD_LANES), lambda b, h, i: (b, 2 * n_h + h)),
                  pl.BlockSpec((ctx_len, HEAD_LANES), lambda b, h, i: (b, n_h + h)),
                  pl.BlockSpec((ctx_len, HEAD_LANES), lambda b, h, i: (b, 2 * n_h + h)),
                  pl.BlockSpec(lam_params.shape, lambda b, h, i: (0, 0)),
                  pl.BlockSpec((1, HEAD_LANES), lambda b, h, i: (0, 0))],
        out_specs=pl.BlockSpec((tq, HEAD_LANES), lambda b, h, i: (b * nq + i, h)),
        out_shape=jax.ShapeDtypeStruct((batch * seq, d_model), BF16),
        compiler_params=_cparams(3),
        name="diff_attention",
    )(qkv, qkv, qkv, qkv_ctx, qkv_ctx, lam_params, subln.reshape(1, HEAD_LANES))


def _oproj_kernel(*refs, route, n_experts):
    if route:
        o_ref, w_ref, x_ref, g1_ref, gain_ref, sh_ref, sc_ref, wr_ref, xo_ref, h_ref, r_ref = refs
    else:
        o_ref, w_ref, x_ref, g1_ref, gain_ref, sh_ref, sc_ref, xo_ref, h_ref = refs
    y = jnp.dot(o_ref[...], w_ref[...], preferred_element_type=F32)
    xn = x_ref[...] + g1_ref[0] * y
    xo_ref[...] = xn
    h = (_rms(xn) * gain_ref[...]) * (1.0 + sc_ref[0]) + sh_ref[0]
    h_hi = h.astype(BF16)
    h_ref[...] = h_hi
    if not route:
        return
    n_exp = n_experts
    h_lo = (h - h_hi.astype(F32)).astype(BF16)
    both = jnp.dot(h_hi, wr_ref[...], preferred_element_type=F32)
    logits = both[:, :HEAD_LANES] + (jnp.dot(h_lo, wr_ref[:, :HEAD_LANES], preferred_element_type=F32)
                                     + both[:, HEAD_LANES:])
    lane = lax.broadcasted_iota(jnp.int32, logits.shape, 1)
    logits = jnp.where(lane < n_exp, logits, -jnp.inf)
    v1 = jnp.max(logits, axis=-1, keepdims=True)
    i1 = jnp.min(jnp.where(logits == v1, lane, n_exp), axis=-1, keepdims=True)
    rest = jnp.where(lane == i1, -jnp.inf, logits)
    v2 = jnp.max(rest, axis=-1, keepdims=True)
    i2 = jnp.min(jnp.where(rest == v2, lane, n_exp), axis=-1, keepdims=True)
    e = jnp.exp(v2 - v1)
    w1 = 1.0 / (1.0 + e)
    w2 = e * w1
    out_lane = lax.broadcasted_iota(jnp.int32, r_ref.shape, 1)
    r_ref[...] = jnp.where(out_lane == 0, i1.astype(F32),
                           jnp.where(out_lane == 1, i2.astype(F32),
                                     jnp.where(out_lane == 2, w1, jnp.where(out_lane == 3, w2, 0.0))))


def oproj_residual_norm(o, w_o_bf16, x, gain, mod, mod_base, rows_per_group, tm, w_router=None, n_experts=0):
    rows, D = x.shape
    tiles_per_group = rows_per_group // tm

    def mod_spec(k):
        return pl.BlockSpec((1, 1, D), lambda i: (mod_base + (i // tiles_per_group) * N_MOD + k, 0, 0))

    in_specs = [pl.BlockSpec((tm, D), lambda i: (i, 0)),
                pl.BlockSpec((D, D), lambda i: (0, 0)),
                pl.BlockSpec((tm, D), lambda i: (i, 0)),
                mod_spec(2),
                pl.BlockSpec((1, D), lambda i: (0, 0)),
                mod_spec(3), mod_spec(4)]
    args = [o, w_o_bf16, x, mod, gain.reshape(1, D), mod, mod]
    out_specs = [pl.BlockSpec((tm, D), lambda i: (i, 0)), pl.BlockSpec((tm, D), lambda i: (i, 0))]
    out_shape = [jax.ShapeDtypeStruct((rows, D), F32),
                 jax.ShapeDtypeStruct((rows, D), BF16)]
    if w_router is not None:
        in_specs.append(pl.BlockSpec(w_router.shape, lambda i: (0, 0)))
        args.append(w_router)
        out_specs.append(pl.BlockSpec((tm, HEAD_LANES), lambda i: (i, 0)))
        out_shape.append(jax.ShapeDtypeStruct((rows, HEAD_LANES), F32))
    return pl.pallas_call(
        functools.partial(_oproj_kernel, route=w_router is not None, n_experts=n_experts),
        grid=(rows // tm,),
        in_specs=in_specs,
        out_specs=out_specs,
        out_shape=out_shape,
        compiler_params=_cparams(1),
        name="oproj_residual_norm",
    )(*args)


def _swiglu_chunk(h, wg, wu, wd):
    a = jnp.dot(h, wg, preferred_element_type=F32)
    b = jnp.dot(h, wu, preferred_element_type=F32)
    t = (a * jax.nn.sigmoid(a)) * b
    return jnp.dot(t.astype(BF16), wd, preferred_element_type=F32)


def _ffn_kernel(h_ref, wg_ref, wu_ref, wd_ref, x_ref, g2_ref, o_ref, acc_ref):
    j = pl.program_id(1)

    @pl.when(j == 0)
    def _():
        acc_ref[...] = jnp.zeros_like(acc_ref)

    acc_ref[...] += _swiglu_chunk(h_ref[...], wg_ref[...], wu_ref[...], wd_ref[...])

    @pl.when(j == pl.num_programs(1) - 1)
    def _():
        o_ref[...] = x_ref[...] + g2_ref[0] * acc_ref[...]


def ffn_residual(h, wg, wu, wd, x, mod, mod_base, rows_per_group, tm, tf):
    rows, D = x.shape
    F = wg.shape[1]
    tiles_per_group = rows_per_group // tm
    return pl.pallas_call(
        _ffn_kernel,
        grid=(rows // tm, F // tf),
        in_specs=[pl.BlockSpec((tm, D), lambda i, j: (i, 0)),
                  pl.BlockSpec((D, tf), lambda i, j: (0, j)),
                  pl.BlockSpec((D, tf), lambda i, j: (0, j)),
                  pl.BlockSpec((tf, D), lambda i, j: (j, 0)),
                  pl.BlockSpec((tm, D), lambda i, j: (i, 0)),
                  pl.BlockSpec((1, 1, D), lambda i, j: (mod_base + (i // tiles_per_group) * N_MOD + 5, 0, 0))],
        out_specs=pl.BlockSpec((tm, D), lambda i, j: (i, 0)),
        out_shape=jax.ShapeDtypeStruct((rows, D), F32),
        scratch_shapes=[pltpu.VMEM((tm, D), F32)],
        compiler_params=_cparams(2),
        name="ffn_residual",
    )(h, wg, wu, wd, x, mod)


SEG_ALIGN = 16
SEG_DMA_ROWS = (512, 256, 128, 64, 32, 16)


def _segment_copies(count_ref, local_ref, global_ref, tile, n_experts, make_copy, action):
    for e in range(n_experts):
        idx = tile * n_experts + e
        count, local, glob = count_ref[idx], local_ref[idx], global_ref[idx]
        for rows in SEG_DMA_ROWS:
            done = count & ~(2 * rows - 1)

            @pl.when((count & rows) != 0)
            def _(rows=rows, done=done, local=local, glob=glob):
                copy = make_copy(pl.multiple_of(local + done, SEG_ALIGN), pl.multiple_of(glob + done, SEG_ALIGN), rows)
                getattr(copy, action)()


def _dispatch_kernel(count_ref, local_ref, global_ref, e1_ref, e2_ref, h_ref, init_ref, o_ref, seg_ref, sem,
                     *, n_experts):
    del init_ref
    tile = pl.program_id(0)
    tm = h_ref.shape[0]
    s_loc = seg_ref.shape[0]
    e1, e2 = e1_ref[0], e2_ref[0]
    expert = lax.broadcasted_iota(jnp.int32, (n_experts, tm), 0)
    member = (expert == e1) | (expert == e2)
    before = (lax.broadcasted_iota(jnp.int32, (tm, tm), 0) < lax.broadcasted_iota(jnp.int32, (tm, tm), 1))
    rank = jnp.dot(jnp.where(member, 1.0, 0.0).astype(BF16), jnp.where(before, 1.0, 0.0).astype(BF16),
                   preferred_element_type=F32)
    local = jnp.zeros((n_experts, 1), jnp.int32)
    for e in range(n_experts):
        local = jnp.where(lax.broadcasted_iota(jnp.int32, (n_experts, 1), 0) == e,
                          local_ref[tile * n_experts + e], local)
    slot = rank + local.astype(F32)
    s1 = jnp.sum(jnp.where(expert == e1, slot, 0.0), axis=0, keepdims=True)
    s2 = jnp.sum(jnp.where(expert == e2, slot, 0.0), axis=0, keepdims=True)
    row = lax.broadcasted_iota(jnp.int32, (s_loc, tm), 0).astype(F32)
    place = jnp.where((row == s1) | (row == s2), 1.0, 0.0).astype(BF16)
    seg_ref[...] = jnp.dot(place, h_ref[...], preferred_element_type=F32).astype(BF16)

    def make_copy(local_row, global_row, rows):
        return pltpu.make_async_copy(seg_ref.at[pl.ds(local_row, rows)], o_ref.at[pl.ds(global_row, rows)], sem)

    _segment_copies(count_ref, local_ref, global_ref, tile, n_experts, make_copy, "start")
    _segment_copies(count_ref, local_ref, global_ref, tile, n_experts, make_copy, "wait")


def dispatch_tokens(h, e1_rows, e2_rows, plan, n_rows, n_experts, tm, s_loc):
    T, D = h.shape
    count, local, glob = plan
    grid_spec = pltpu.PrefetchScalarGridSpec(
        num_scalar_prefetch=3,
        grid=(T // tm,),
        in_specs=[pl.BlockSpec((1, 1, tm), lambda t, *_: (t, 0, 0)),
                  pl.BlockSpec((1, 1, tm), lambda t, *_: (t, 0, 0)),
                  pl.BlockSpec((tm, D), lambda t, *_: (t, 0)),
                  pl.BlockSpec(memory_space=pl.ANY)],
        out_specs=pl.BlockSpec(memory_space=pl.ANY),
        scratch_shapes=[pltpu.VMEM((s_loc, D), BF16), pltpu.SemaphoreType.DMA(())],
    )
    return pl.pallas_call(
        functools.partial(_dispatch_kernel, n_experts=n_experts),
        grid_spec=grid_spec,
        out_shape=jax.ShapeDtypeStruct((n_rows, D), BF16),
        input_output_aliases={6: 0},
        compiler_params=_cparams(1),
        name="dispatch_tokens",
    )(count, local, glob, e1_rows, e2_rows, h, jnp.zeros((n_rows, D), BF16))


def _expert_ffn_kernel(te_ref, valid_ref, h_ref, wg_ref, wu_ref, wd_ref, o_ref, acc_ref):
    i, j = pl.program_id(0), pl.program_id(1)
    last = pl.num_programs(1) - 1

    @pl.when(j == 0)
    def _():
        acc_ref[...] = jnp.zeros_like(acc_ref)

    @pl.when(valid_ref[i] > 0)
    def _():
        acc_ref[...] += _swiglu_chunk(h_ref[...], wg_ref[...], wu_ref[...], wd_ref[...])

    @pl.when(j == last)
    def _():
        o_ref[...] = acc_ref[...].astype(BF16)


def expert_ffn(h_sorted, tile_expert, tile_valid, wg, wu, wd, tm, tf):
    rows, D = h_sorted.shape
    F = wg.shape[2]
    grid_spec = pltpu.PrefetchScalarGridSpec(
        num_scalar_prefetch=2,
        grid=(rows // tm, F // tf),
        in_specs=[pl.BlockSpec((tm, D), lambda i, j, te, va: (i, 0)),
                  pl.BlockSpec((None, D, tf), lambda i, j, te, va: (te[i], 0, j)),
                  pl.BlockSpec((None, D, tf), lambda i, j, te, va: (te[i], 0, j)),
                  pl.BlockSpec((None, tf, D), lambda i, j, te, va: (te[i], j, 0))],
        out_specs=pl.BlockSpec((tm, D), lambda i, j, te, va: (i, 0)),
        scratch_shapes=[pltpu.VMEM((tm, D), F32)],
    )
    return pl.pallas_call(
        _expert_ffn_kernel,
        grid_spec=grid_spec,
        out_shape=jax.ShapeDtypeStruct((rows, D), BF16),
        compiler_params=_cparams(2),
        name="expert_ffn",
    )(tile_expert, tile_valid, h_sorted, wg, wu, wd)


def _combine_kernel(count_ref, local_ref, global_ref, y_ref, route_ref, x_ref, g2_ref, gain_ref, o_ref, buf_ref, sem,
                    *, n_experts):
    tile = pl.program_id(0)
    tm = o_ref.shape[0]
    s_loc = buf_ref.shape[0]
    buf_ref[...] = jnp.zeros_like(buf_ref)

    def make_copy(local_row, global_row, rows):
        return pltpu.make_async_copy(y_ref.at[pl.ds(global_row, rows)], buf_ref.at[pl.ds(local_row, rows)], sem)

    _segment_copies(count_ref, local_ref, global_ref, tile, n_experts, make_copy, "start")
    route = route_ref[...]
    e1 = route[:, 0:1].astype(jnp.int32)
    e2 = route[:, 1:2].astype(jnp.int32)
    expert = lax.broadcasted_iota(jnp.int32, (tm, HEAD_LANES), 1)
    member = (expert == e1) | (expert == e2)
    before = (lax.broadcasted_iota(jnp.int32, (tm, tm), 1) < lax.broadcasted_iota(jnp.int32, (tm, tm), 0))
    rank = jnp.dot(jnp.where(before, 1.0, 0.0).astype(BF16), jnp.where(member, 1.0, 0.0).astype(BF16),
                   preferred_element_type=F32)
    local = jnp.zeros((1, HEAD_LANES), jnp.int32)
    for e in range(n_experts):
        local = jnp.where(lax.broadcasted_iota(jnp.int32, (1, HEAD_LANES), 1) == e,
                          local_ref[tile * n_experts + e], local)
    slot = rank + local.astype(F32)
    s1 = jnp.sum(jnp.where(expert == e1, slot, 0.0), axis=1, keepdims=True)
    s2 = jnp.sum(jnp.where(expert == e2, slot, 0.0), axis=1, keepdims=True)
    col = lax.broadcasted_iota(jnp.int32, (tm, s_loc), 1).astype(F32)
    pick1 = jnp.where(col == s1, 1.0, 0.0).astype(BF16)
    pick2 = jnp.where(col == s2, 1.0, 0.0).astype(BF16)
    _segment_copies(count_ref, local_ref, global_ref, tile, n_experts, make_copy, "wait")
    y_loc = buf_ref[...]
    y = (route[:, 2:3] * jnp.dot(pick1, y_loc, preferred_element_type=F32)
         + route[:, 3:4] * jnp.dot(pick2, y_loc, preferred_element_type=F32))
    xn = x_ref[...] + g2_ref[0] * y
    o_ref[...] = _rms(xn) * gain_ref[...]


def combine_final(y_sorted, plan, route, x, mod, mod_base, rows_per_group, final_gain, n_experts, tm, s_loc):
    rows, D = x.shape
    count, local, glob = plan
    tiles_per_group = rows_per_group // tm
    grid_spec = pltpu.PrefetchScalarGridSpec(
        num_scalar_prefetch=3,
        grid=(rows // tm,),
        in_specs=[pl.BlockSpec(memory_space=pl.ANY),
                  pl.BlockSpec((tm, HEAD_LANES), lambda i, *_: (i, 0)),
                  pl.BlockSpec((tm, D), lambda i, *_: (i, 0)),
                  pl.BlockSpec((1, 1, D), lambda i, *_: (mod_base + (i // tiles_per_group) * N_MOD + 5, 0, 0)),
                  pl.BlockSpec((1, D), lambda i, *_: (0, 0))],
        out_specs=pl.BlockSpec((tm, D), lambda i, *_: (i, 0)),
        scratch_shapes=[pltpu.VMEM((s_loc, D), BF16), pltpu.SemaphoreType.DMA(())],
    )
    return pl.pallas_call(
        functools.partial(_combine_kernel, n_experts=n_experts),
        grid_spec=grid_spec,
        out_shape=jax.ShapeDtypeStruct((rows, D), F32),
        compiler_params=_cparams(1),
        name="combine_final",
    )(count, local, glob, y_sorted, route, x, mod, final_gain.reshape(1, D))


def routing_plan(route, n_experts, tm):
    T = route.shape[0]
    n_tok_tiles = T // tm
    e1 = route[:, 0].astype(jnp.int32)
    e2 = route[:, 1].astype(jnp.int32)
    experts = jnp.arange(n_experts)[None, :]
    member = ((e1[:, None] == experts) | (e2[:, None] == experts)).astype(jnp.int32)
    count = jnp.sum(member.reshape(n_tok_tiles, tm, n_experts), axis=1)
    count = ((count + SEG_ALIGN - 1) // SEG_ALIGN) * SEG_ALIGN
    local = jnp.cumsum(count, axis=1) - count
    total = jnp.sum(count, axis=0)
    padded = ((total + tm - 1) // tm) * tm
    ends = jnp.cumsum(padded)
    glob = (ends - padded)[None, :] + jnp.cumsum(count, axis=0) - count
    n_tiles = (TOP_K * T + n_tok_tiles * n_experts * (SEG_ALIGN - 1) + n_experts * (tm - 1)) // tm + 1
    tile_start = jnp.arange(n_tiles) * tm
    tile_valid = (tile_start < ends[-1]).astype(jnp.int32)
    tile_expert = jnp.sum((tile_start[:, None] >= ends[None, :]).astype(jnp.int32), axis=1)
    last_expert = jnp.max(jnp.where(tile_valid > 0, tile_expert, 0))
    tile_expert = jnp.where(tile_valid > 0, tile_expert, last_expert)
    plan = tuple(a.reshape(-1).astype(jnp.int32) for a in (count, local, glob))
    return plan, e1.reshape(n_tok_tiles, 1, tm), e2.reshape(n_tok_tiles, 1, tm), tile_expert, tile_valid, n_tiles * tm


def kernel(x, c, ctx, c_ctx, l0_w_ada, l0_b_ada, l0_norm_mix, l0_w_qkv, l0_rpb, l0_w_o, l0_norm_ffn, l0_w_gate, l0_w_up, l0_w_down, l1_w_ada, l1_b_ada, l1_norm_mix, l1_w_qkv, l1_lambda_q1, l1_lambda_k1, l1_lambda_q2, l1_lambda_k2, l1_subln, l1_w_o, l1_norm_ffn, l1_w_router, l1_w_gate, l1_w_up, l1_w_down, final_norm):
    B, S, D = x.shape
    C = ctx.shape[1]
    T, TC = B * S, B * C
    n_experts = l1_w_router.shape[1]
    tm = min(512, S)
    tmc = min(512, TC)

    xf = x.reshape(T, D)
    cf = ctx.reshape(TC, D)

    n_cond = ((B + 1 + 7) // 8) * 8
    cond = jnp.zeros((n_cond, D), F32).at[:B].set(c).at[B].set(c_ctx)
    ctx_base = B * N_MOD

    def qkv_weight(w, scale):
        col_scale = jnp.concatenate([jnp.full((D,), scale, F32), jnp.ones((2 * D,), F32)])
        return (w * col_scale[None, :]).astype(BF16)

    mod0 = ada_params(cond, l0_w_ada, l0_b_ada)
    w_qkv0 = qkv_weight(l0_w_qkv, (D // NA_HEADS) ** -0.5)
    qkv = norm_mod_qkv(xf, l0_norm_mix, mod0, 0, S, w_qkv0, tm)
    qkv_c = norm_mod_qkv(cf, l0_norm_mix, mod0, ctx_base, TC, w_qkv0, tmc)
    bias = na_bias(l0_rpb, S // GRID_W)
    o = neighbourhood_attention(qkv, qkv_c, bias, B, S, C, D)
    oc = ctx_attention(qkv_c, B, C, D)
    w_o0 = l0_w_o.astype(BF16)
    xf, h = oproj_residual_norm(o, w_o0, xf, l0_norm_ffn, mod0, 0, S, tm)
    cf, hc = oproj_residual_norm(oc, w_o0, cf, l0_norm_ffn, mod0, ctx_base, TC, tmc)
    wg0, wu0, wd0 = l0_w_gate.astype(BF16), l0_w_up.astype(BF16), l0_w_down.astype(BF16)
    d_ff = l0_w_gate.shape[1]
    tf0 = d_ff // 2 if (d_ff // 2) % 128 == 0 else d_ff
    xf = ffn_residual(h, wg0, wu0, wd0, xf, mod0, 0, S, tm, tf0)
    cf = ffn_residual(hc, wg0, wu0, wd0, cf, mod0, ctx_base, TC, tmc, tf0)

    mod1 = ada_params(cond, l1_w_ada, l1_b_ada)
    w_qkv1 = qkv_weight(l1_w_qkv, DIFF_HEAD_DIM ** -0.5)
    qkv = norm_mod_qkv(xf, l1_norm_mix, mod1, 0, S, w_qkv1, tm, rope_tables=rope_tables(S))
    qkv_c = norm_mod_qkv(cf, l1_norm_mix, mod1, ctx_base, TC, w_qkv1, tmc)
    lam_init = 0.8 - 0.6 * math.exp(-0.3 * 1)
    lam_params = jnp.stack([l1_lambda_q1, l1_lambda_k1, l1_lambda_q2, l1_lambda_k2]).astype(F32)
    o = diff_attention(qkv, qkv_c, lam_params, l1_subln, lam_init, B, S, C, D, tq=min(512, S))
    w_router = jnp.zeros((D, HEAD_LANES), F32).at[:, :n_experts].set(l1_w_router)
    w_router_hi = w_router.astype(BF16)
    w_router = jnp.concatenate([w_router_hi, (w_router - w_router_hi.astype(F32)).astype(BF16)], axis=1)
    xf, h, route = oproj_residual_norm(o, l1_w_o.astype(BF16), xf, l1_norm_ffn, mod1, 0, S, tm,
                                       w_router=w_router, n_experts=n_experts)
    tme = min(512, S)
    s_loc = -(-(TOP_K * tme + n_experts * (SEG_ALIGN - 1)) // HEAD_LANES) * HEAD_LANES
    plan, e1_rows, e2_rows, tile_expert, tile_valid, n_rows = routing_plan(route, n_experts, tme)
    h_sorted = dispatch_tokens(h, e1_rows, e2_rows, plan, n_rows, n_experts, tme, s_loc)
    d_ffe = l1_w_gate.shape[2]
    tfe = d_ffe // 2 if (d_ffe // 2) % 128 == 0 else d_ffe
    y_sorted = expert_ffn(h_sorted, tile_expert, tile_valid, l1_w_gate.astype(BF16), l1_w_up.astype(BF16),
                          l1_w_down.astype(BF16), tme, tfe)
    out = combine_final(y_sorted, plan, route, xf, mod1, 0, S, final_norm, n_experts, tme, s_loc)
    return out.reshape(B, S, D)
```

```python
import functools
import math

import jax
import jax.numpy as jnp
from jax import lax
from jax.experimental import pallas as pl
from jax.experimental.pallas import tpu as pltpu

F32 = jnp.float32
BF16 = jnp.bfloat16

GRID_W = 64
NA_HEADS = 16
NA_WIN_R = 8
NA_WIN_C = 16
NA_QROWS = 8
NA_KROWS = 16
NA_STEP_BLOCKS = 2
DIFF_HEADS = 8
DIFF_HEAD_DIM = 64
ROPE_THETA = 10000.0
TOP_K = 2
NORM_EPS = 1e-6
HEAD_LANES = 128
MASK_VALUE = -1e30
N_MOD = 6
VMEM_LIMIT = 56 * 1024 * 1024


def _cparams(n_axes, vmem=VMEM_LIMIT):
    return pltpu.CompilerParams(dimension_semantics=("arbitrary",) * n_axes, vmem_limit_bytes=vmem)


def _rms(x):
    return x * lax.rsqrt(jnp.mean(x * x, axis=-1, keepdims=True) + NORM_EPS)


def _ada_kernel(c_ref, w_ref, b_ref, o_ref):
    c = c_ref[...]
    s = c * jax.nn.sigmoid(c)
    o_ref[...] = jnp.dot(s, w_ref[...], preferred_element_type=F32,
                         precision=lax.Precision.HIGHEST) + b_ref[...]


def ada_params(cond, w_ada, b_ada):
    R, D = cond.shape
    N = w_ada.shape[1]
    tn = N // 4
    out = pl.pallas_call(
        _ada_kernel,
        grid=(N // tn,),
        in_specs=[pl.BlockSpec((R, D), lambda j: (0, 0)),
                  pl.BlockSpec((D, tn), lambda j: (0, j)),
                  pl.BlockSpec((1, tn), lambda j: (0, j))],
        out_specs=pl.BlockSpec((R, tn), lambda j: (0, j)),
        out_shape=jax.ShapeDtypeStruct((R, N), F32),
        compiler_params=_cparams(1),
        name="ada_params",
    )(cond, w_ada, b_ada.reshape(1, N))
    return out.reshape(R * N_MOD, 1, D)


def _qkv_kernel(*refs, rope, d_model):
    if rope:
        x_ref, g_ref, sh_ref, sc_ref, w_ref, cos_ref, sa_ref, sb_ref, o_ref = refs
    else:
        x_ref, g_ref, sh_ref, sc_ref, w_ref, o_ref = refs
    h = (_rms(x_ref[...]) * g_ref[...]) * (1.0 + sc_ref[0]) + sh_ref[0]
    hb = h.astype(BF16)
    n_chunks = w_ref.shape[1] // d_model
    for n in range(n_chunks):
        acc = jnp.dot(hb, w_ref[:, n * d_model:(n + 1) * d_model], preferred_element_type=F32)
        if rope and n < 2:
            cos, sa, sb = cos_ref[...], sa_ref[...], sb_ref[...]
            for s in range(d_model // HEAD_LANES):
                xs = acc[:, s * HEAD_LANES:(s + 1) * HEAD_LANES]
                rot = xs * cos + pltpu.roll(xs, HEAD_LANES - 16, 1) * sa + pltpu.roll(xs, 16, 1) * sb
                col = n * d_model + s * HEAD_LANES
                o_ref[:, col:col + HEAD_LANES] = rot.astype(BF16)
        else:
            o_ref[:, n * d_model:(n + 1) * d_model] = acc.astype(BF16)


def norm_mod_qkv(x, gain, mod, mod_base, rows_per_group, w_bf16, tm, rope_tables=None):
    rows, D = x.shape
    N = w_bf16.shape[1]
    tiles_per_group = rows_per_group // tm
    in_specs = [pl.BlockSpec((tm, D), lambda i: (i, 0)),
                pl.BlockSpec((1, D), lambda i: (0, 0)),
                pl.BlockSpec((1, 1, D), lambda i: (mod_base + (i // tiles_per_group) * N_MOD + 0, 0, 0)),
                pl.BlockSpec((1, 1, D), lambda i: (mod_base + (i // tiles_per_group) * N_MOD + 1, 0, 0)),
                pl.BlockSpec((D, N), lambda i: (0, 0))]
    args = [x, gain.reshape(1, D), mod, mod, w_bf16]
    if rope_tables is not None:
        seq_tiles = rope_tables[0].shape[0] // tm
        for t in rope_tables:
            in_specs.append(pl.BlockSpec((tm, HEAD_LANES), lambda i: (i % seq_tiles, 0)))
            args.append(t)
    return pl.pallas_call(
        functools.partial(_qkv_kernel, rope=rope_tables is not None, d_model=D),
        grid=(rows // tm,),
        in_specs=in_specs,
        out_specs=pl.BlockSpec((tm, N), lambda i: (i, 0)),
        out_shape=jax.ShapeDtypeStruct((rows, N), BF16),
        compiler_params=_cparams(1),
        name="norm_mod_qkv",
    )(*args)


def rope_tables(seq):
    half = DIFF_HEAD_DIM // 2
    n_freq = half // 2
    inv_freq = ROPE_THETA ** (-jnp.arange(0, half, 2, dtype=F32) / half)
    pos = jnp.arange(seq)
    rows_pos = (pos // GRID_W).astype(F32)
    cols_pos = (pos % GRID_W).astype(F32)
    lane = jnp.arange(HEAD_LANES)
    d = lane % DIFF_HEAD_DIM
    use_col = (d // half) == 1
    j = d % half
    freq = inv_freq[j % n_freq]
    ang = jnp.where(use_col[None, :], cols_pos[:, None], rows_pos[:, None]) * freq[None, :]
    cos, sin = jnp.cos(ang), jnp.sin(ang)
    first = (j < n_freq)[None, :]
    return cos, jnp.where(first, -sin, 0.0), jnp.where(first, 0.0, sin)


def _head_mask(q, head):
    lane = lax.broadcasted_iota(jnp.int32, (1, HEAD_LANES), 1)
    return jnp.where((lane // 64) == head, q, jnp.zeros_like(q))


def _qk(q, k):
    return lax.dot_general(q, k, (((1,), (1,)), ((), ())), preferred_element_type=F32)


def _softmax2(s_a, s_b):
    m = jnp.maximum(jnp.max(s_a, axis=-1, keepdims=True), jnp.max(s_b, axis=-1, keepdims=True))
    p_a = jnp.exp(s_a - m)
    p_b = jnp.exp(s_b - m)
    denom = jnp.sum(p_a, axis=-1, keepdims=True) + jnp.sum(p_b, axis=-1, keepdims=True)
    return p_a, p_b, 1.0 / denom


def _na_kernel(q_ref, k_ref, v_ref, kc_ref, vc_ref, bias_ref, o_ref, *, grid_rows):
    g = pl.program_id(0)
    k_row0 = jnp.clip(g * NA_QROWS - NA_WIN_R // 2, 0, grid_rows - NA_KROWS) * GRID_W
    k_row0 = pl.multiple_of(k_row0, GRID_W)
    n_keys = NA_KROWS * GRID_W
    lane = lax.broadcasted_iota(jnp.int32, (1, HEAD_LANES), 1)
    for blk in range(q_ref.shape[1] // HEAD_LANES):
        cols = slice(blk * HEAD_LANES, (blk + 1) * HEAD_LANES)
        q = q_ref[:, cols]
        k_lat = k_ref[pl.ds(k_row0, n_keys), cols]
        v_lat = v_ref[pl.ds(k_row0, n_keys), cols]
        kc, vc = kc_ref[:, cols], vc_ref[:, cols]
        outs = []
        for head in range(2):
            qm = _head_mask(q, head)
            s_lat = _qk(qm, k_lat) + bias_ref[0, 2 * blk + head]
            s_ctx = _qk(qm, kc)
            p_lat, p_ctx, inv = _softmax2(s_lat, s_ctx)
            o = (jnp.dot(p_lat.astype(BF16), v_lat, preferred_element_type=F32)
                 + jnp.dot(p_ctx.astype(BF16), vc, preferred_element_type=F32))
            outs.append(o * inv)
        o_ref[:, cols] = jnp.where(lane < 64, outs[0], outs[1]).astype(BF16)


NA_ROW_OFFSETS = 2 * NA_WIN_R - 1
NA_PAIR_ENTRIES = NA_ROW_OFFSETS + 3


def na_pair_table(rpb):
    H = rpb.shape[0]
    qc = jnp.arange(GRID_W)[:, None]
    kcol = jnp.arange(GRID_W)[None, :]
    ws = jnp.clip(qc - NA_WIN_C // 2, 0, GRID_W - NA_WIN_C)
    col_ok = (kcol >= ws) & (kcol < ws + NA_WIN_C)
    dc = jnp.clip(kcol - qc, -(NA_WIN_C - 1), NA_WIN_C - 1) + (NA_WIN_C - 1)
    blocks = jnp.where(col_ok[None, None], rpb[:, :, dc], MASK_VALUE)
    pad = jnp.full((H, 2, GRID_W, GRID_W), MASK_VALUE, F32)
    padded = jnp.concatenate([pad, blocks, pad], axis=1)
    return jnp.concatenate([padded[:, :-1], padded[:, 1:]], axis=-1)


def _na_bias_kernel(tp_ref, o_ref, *, grid_rows):
    g = pl.program_id(0)
    k_row0 = jnp.clip(g * NA_QROWS - NA_WIN_R // 2, 0, grid_rows - NA_KROWS)
    lane = lax.broadcasted_iota(jnp.int32, (1, HEAD_LANES), 1)
    for qr in range(NA_QROWS):
        r = g * NA_QROWS + qr
        rs = jnp.clip(r - NA_WIN_R // 2, 0, grid_rows - NA_WIN_R)
        for pair in range(NA_KROWS // 2):
            kr = k_row0 + 2 * pair
            ok0 = ((kr >= rs) & (kr < rs + NA_WIN_R)).astype(jnp.int32)
            ok1 = ((kr + 1 >= rs) & (kr + 1 < rs + NA_WIN_R)).astype(jnp.int32)
            entry = jnp.clip(kr - r + (NA_WIN_R - 1) + 2, 0, NA_PAIR_ENTRIES - 1)
            ok = jnp.where(lane < GRID_W, ok0, ok1) > 0
            o_ref[0, 0, qr * GRID_W:(qr + 1) * GRID_W, pair * HEAD_LANES:(pair + 1) * HEAD_LANES] = (
                jnp.where(ok, tp_ref[0, entry], MASK_VALUE))


def na_bias(rpb, grid_rows):
    H = rpb.shape[0]
    n_groups = grid_rows // NA_QROWS
    tq, nk = NA_QROWS * GRID_W, NA_KROWS * GRID_W
    return pl.pallas_call(
        functools.partial(_na_bias_kernel, grid_rows=grid_rows),
        grid=(n_groups, H),
        in_specs=[pl.BlockSpec((1, NA_PAIR_ENTRIES, GRID_W, HEAD_LANES), lambda g, h: (h, 0, 0, 0))],
        out_specs=pl.BlockSpec((1, 1, tq, nk), lambda g, h: (g, h, 0, 0)),
        out_shape=jax.ShapeDtypeStruct((n_groups, H, tq, nk), F32),
        compiler_params=_cparams(2),
        name="na_bias",
    )(na_pair_table(rpb))


def neighbourhood_attention(qkv, qkv_ctx, bias, batch, seq, ctx_len, d_model):
    grid_rows = seq // GRID_W
    n_groups = grid_rows // NA_QROWS
    tq = NA_QROWS * GRID_W
    width = NA_STEP_BLOCKS * HEAD_LANES
    n_hp = d_model // width
    nk = NA_KROWS * GRID_W
    return pl.pallas_call(
        functools.partial(_na_kernel, grid_rows=grid_rows),
        grid=(n_groups, n_hp, batch),
        in_specs=[pl.BlockSpec((tq, width), lambda g, h, b: (b * n_groups + g, h)),
                  pl.BlockSpec((seq, width), lambda g, h, b: (b, n_hp + h)),
                  pl.BlockSpec((seq, width), lambda g, h, b: (b, 2 * n_hp + h)),
                  pl.BlockSpec((ctx_len, width), lambda g, h, b: (b, n_hp + h)),
                  pl.BlockSpec((ctx_len, width), lambda g, h, b: (b, 2 * n_hp + h)),
                  pl.BlockSpec((1, 2 * NA_STEP_BLOCKS, tq, nk), lambda g, h, b: (g, h, 0, 0))],
        out_specs=pl.BlockSpec((tq, width), lambda g, h, b: (b * n_groups + g, h)),
        out_shape=jax.ShapeDtypeStruct((batch * seq, d_model), BF16),
        compiler_params=_cparams(3),
        name="neighbourhood_attention",
    )(qkv, qkv, qkv, qkv_ctx, qkv_ctx, bias)


def _ctx_attn_kernel(q_ref, k_ref, v_ref, o_ref):
    lane = lax.broadcasted_iota(jnp.int32, (1, HEAD_LANES), 1)
    for blk in range(q_ref.shape[1] // HEAD_LANES):
        cols = slice(blk * HEAD_LANES, (blk + 1) * HEAD_LANES)
        q, k, v = q_ref[:, cols], k_ref[:, cols], v_ref[:, cols]
        outs = []
        for head in range(2):
            s = _qk(_head_mask(q, head), k)
            m = jnp.max(s, axis=-1, keepdims=True)
            p = jnp.exp(s - m)
            inv = 1.0 / jnp.sum(p, axis=-1, keepdims=True)
            outs.append(jnp.dot(p.astype(BF16), v, preferred_element_type=F32) * inv)
        o_ref[:, cols] = jnp.where(lane < 64, outs[0], outs[1]).astype(BF16)


def ctx_attention(qkv_ctx, batch, ctx_len, d_model):
    return pl.pallas_call(
        _ctx_attn_kernel,
        grid=(batch,),
        in_specs=[pl.BlockSpec((ctx_len, d_model), lambda b: (b, 0)),
                  pl.BlockSpec((ctx_len, d_model), lambda b: (b, 1)),
                  pl.BlockSpec((ctx_len, d_model), lambda b: (b, 2))],
        out_specs=pl.BlockSpec((ctx_len, d_model), lambda b: (b, 0)),
        out_shape=jax.ShapeDtypeStruct((batch * ctx_len, d_model), BF16),
        compiler_params=_cparams(1),
        name="ctx_attention",
    )(qkv_ctx, qkv_ctx, qkv_ctx)


def _diff_attn_kernel(q_ref, k_ref, v_ref, kc_ref, vc_ref, lam_ref, subln_ref, o_ref, *, lam_init, sub_rows):
    lp = lam_ref[...]
    lam = (jnp.exp(jnp.sum(lp[0:1] * lp[1:2], axis=-1, keepdims=True))
           - jnp.exp(jnp.sum(lp[2:3] * lp[3:4], axis=-1, keepdims=True)) + lam_init)
    k, v, kc, vc = k_ref[...], v_ref[...], kc_ref[...], vc_ref[...]

    def attend(qm):
        p_lat, p_ctx, inv = _softmax2(_qk(qm, k), _qk(qm, kc))
        return (jnp.dot(p_lat.astype(BF16), v, preferred_element_type=F32)
                + jnp.dot(p_ctx.astype(BF16), vc, preferred_element_type=F32)), inv

    for r0 in range(0, q_ref.shape[0], sub_rows):
        q = q_ref[r0:r0 + sub_rows, :]
        o1, inv1 = attend(_head_mask(q, 0))
        o2, inv2 = attend(_head_mask(q, 1))
        o = o1 * inv1 - o2 * (lam * inv2)
        o_ref[r0:r0 + sub_rows, :] = ((_rms(o) * subln_ref[...]) * (1.0 - lam_init)).astype(BF16)


def diff_attention(qkv, qkv_ctx, lam_params, subln, lam_init, batch, seq, ctx_len, d_model, tq, sub_rows=256):
    n_h = d_model // HEAD_LANES
    nq = seq // tq
    return pl.pallas_call(
        functools.partial(_diff_attn_kernel, lam_init=lam_init, sub_rows=min(sub_rows, tq)),
        grid=(batch, n_h, nq),
        in_specs=[pl.BlockSpec((tq, HEAD_LANES), lambda b, h, i: (b * nq + i, h)),
                  pl.BlockSpec((seq, HEAD_LANES), lambda b, h, i: (b, n_h + h)),
                  pl.BlockSpec((seq, HEAD_LANES), lambda b, h, i: (b, 2 * n_h + h)),
                  pl.BlockSpec((ctx_len, HEAD_LANES), lambda b, h, i: (b, n_h + h)),
                  pl.BlockSpec((ctx_len, HEAD_LANES), lambda b, h, i: (b, 2 * n_h + h)),
                  pl.BlockSpec(lam_params.shape, lambda b, h, i: (0, 0)),
                  pl.BlockSpec((1, HEAD_LANES), lambda b, h, i: (0, 0))],
        out_specs=pl.BlockSpec((tq, HEAD_LANES), lambda b, h, i: (b * nq + i, h)),
        out_shape=jax.ShapeDtypeStruct((batch * seq, d_model), BF16),
        compiler_params=_cparams(3),
        name="diff_attention",
    )(qkv, qkv, qkv, qkv_ctx, qkv_ctx, lam_params, subln.reshape(1, HEAD_LANES))


def _oproj_kernel(*refs, route, n_experts):
    if route:
        o_ref, w_ref, x_ref, g1_ref, gain_ref, sh_ref, sc_ref, wr_ref, xo_ref, h_ref, r_ref = refs
    else:
        o_ref, w_ref, x_ref, g1_ref, gain_ref, sh_ref, sc_ref, xo_ref, h_ref = refs
    y = jnp.dot(o_ref[...], w_ref[...], preferred_element_type=F32)
    xn = x_ref[...] + g1_ref[0] * y
    xo_ref[...] = xn
    h = (_rms(xn) * gain_ref[...]) * (1.0 + sc_ref[0]) + sh_ref[0]
    h_hi = h.astype(BF16)
    h_ref[...] = h_hi
    if not route:
        return
    n_exp = n_experts
    h_lo = (h - h_hi.astype(F32)).astype(BF16)
    both = jnp.dot(h_hi, wr_ref[...], preferred_element_type=F32)
    logits = both[:, :HEAD_LANES] + (jnp.dot(h_lo, wr_ref[:, :HEAD_LANES], preferred_element_type=F32)
                                     + both[:, HEAD_LANES:])
    lane = lax.broadcasted_iota(jnp.int32, logits.shape, 1)
    logits = jnp.where(lane < n_exp, logits, -jnp.inf)
    v1 = jnp.max(logits, axis=-1, keepdims=True)
    i1 = jnp.min(jnp.where(logits == v1, lane, n_exp), axis=-1, keepdims=True)
    rest = jnp.where(lane == i1, -jnp.inf, logits)
    v2 = jnp.max(rest, axis=-1, keepdims=True)
    i2 = jnp.min(jnp.where(rest == v2, lane, n_exp), axis=-1, keepdims=True)
    e = jnp.exp(v2 - v1)
    w1 = 1.0 / (1.0 + e)
    w2 = e * w1
    out_lane = lax.broadcasted_iota(jnp.int32, r_ref.shape, 1)
    r_ref[...] = jnp.where(out_lane == 0, i1.astype(F32),
                           jnp.where(out_lane == 1, i2.astype(F32),
                                     jnp.where(out_lane == 2, w1, jnp.where(out_lane == 3, w2, 0.0))))


def oproj_residual_norm(o, w_o_bf16, x, gain, mod, mod_base, rows_per_group, tm, w_router=None, n_experts=0):
    rows, D = x.shape
    tiles_per_group = rows_per_group // tm

    def mod_spec(k):
        return pl.BlockSpec((1, 1, D), lambda i: (mod_base + (i // tiles_per_group) * N_MOD + k, 0, 0))

    in_specs = [pl.BlockSpec((tm, D), lambda i: (i, 0)),
                pl.BlockSpec((D, D), lambda i: (0, 0)),
                pl.BlockSpec((tm, D), lambda i: (i, 0)),
                mod_spec(2),
                pl.BlockSpec((1, D), lambda i: (0, 0)),
                mod_spec(3), mod_spec(4)]
    args = [o, w_o_bf16, x, mod, gain.reshape(1, D), mod, mod]
    out_specs = [pl.BlockSpec((tm, D), lambda i: (i, 0)), pl.BlockSpec((tm, D), lambda i: (i, 0))]
    out_shape = [jax.ShapeDtypeStruct((rows, D), F32),
                 jax.ShapeDtypeStruct((rows, D), BF16)]
    if w_router is not None:
        in_specs.append(pl.BlockSpec(w_router.shape, lambda i: (0, 0)))
        args.append(w_router)
        out_specs.append(pl.BlockSpec((tm, HEAD_LANES), lambda i: (i, 0)))
        out_shape.append(jax.ShapeDtypeStruct((rows, HEAD_LANES), F32))
    return pl.pallas_call(
        functools.partial(_oproj_kernel, route=w_router is not None, n_experts=n_experts),
        grid=(rows // tm,),
        in_specs=in_specs,
        out_specs=out_specs,
        out_shape=out_shape,
        compiler_params=_cparams(1),
        name="oproj_residual_norm",
    )(*args)


def _swiglu_chunk(h, wg, wu, wd):
    a = jnp.dot(h, wg, preferred_element_type=F32)
    b = jnp.dot(h, wu, preferred_element_type=F32)
    t = (a * jax.nn.sigmoid(a)) * b
    return jnp.dot(t.astype(BF16), wd, preferred_element_type=F32)


def _ffn_kernel(h_ref, wg_ref, wu_ref, wd_ref, x_ref, g2_ref, o_ref, acc_ref):
    j = pl.program_id(1)

    @pl.when(j == 0)
    def _():
        acc_ref[...] = jnp.zeros_like(acc_ref)

    acc_ref[...] += _swiglu_chunk(h_ref[...], wg_ref[...], wu_ref[...], wd_ref[...])

    @pl.when(j == pl.num_programs(1) - 1)
    def _():
        o_ref[...] = x_ref[...] + g2_ref[0] * acc_ref[...]


def ffn_residual(h, wg, wu, wd, x, mod, mod_base, rows_per_group, tm, tf):
    rows, D = x.shape
    F = wg.shape[1]
    tiles_per_group = rows_per_group // tm
    w_mode = dict(pipeline_mode=pl.Buffered(1)) if tf == F else {}
    return pl.pallas_call(
        _ffn_kernel,
        grid=(rows // tm, F // tf),
        in_specs=[pl.BlockSpec((tm, D), lambda i, j: (i, 0)),
                  pl.BlockSpec((D, tf), lambda i, j: (0, j), **w_mode),
                  pl.BlockSpec((D, tf), lambda i, j: (0, j), **w_mode),
                  pl.BlockSpec((tf, D), lambda i, j: (j, 0), **w_mode),
                  pl.BlockSpec((tm, D), lambda i, j: (i, 0)),
                  pl.BlockSpec((1, 1, D), lambda i, j: (mod_base + (i // tiles_per_group) * N_MOD + 5, 0, 0))],
        out_specs=pl.BlockSpec((tm, D), lambda i, j: (i, 0)),
        out_shape=jax.ShapeDtypeStruct((rows, D), F32),
        scratch_shapes=[pltpu.VMEM((tm, D), F32)],
        compiler_params=_cparams(2),
        name="ffn_residual",
    )(h, wg, wu, wd, x, mod)


SEG_ALIGN = 16
SEG_DMA_ROWS = (512, 256, 128, 64, 32, 16)


def _segment_copies(count_ref, local_ref, global_ref, tile, n_experts, make_copy, action):
    for e in range(n_experts):
        idx = tile * n_experts + e
        count, local, glob = count_ref[idx], local_ref[idx], global_ref[idx]
        for rows in SEG_DMA_ROWS:
            done = count & ~(2 * rows - 1)

            @pl.when((count & rows) != 0)
            def _(rows=rows, done=done, local=local, glob=glob):
                copy = make_copy(pl.multiple_of(local + done, SEG_ALIGN), pl.multiple_of(glob + done, SEG_ALIGN), rows)
                getattr(copy, action)()


def _dispatch_kernel(count_ref, local_ref, global_ref, e1_ref, e2_ref, h_ref, init_ref, o_ref, seg_ref, sem,
                     *, n_experts):
    del init_ref
    tile = pl.program_id(0)
    tm = h_ref.shape[0]
    s_loc = seg_ref.shape[0]
    e1, e2 = e1_ref[0], e2_ref[0]
    expert = lax.broadcasted_iota(jnp.int32, (n_experts, tm), 0)
    member = (expert == e1) | (expert == e2)
    before = (lax.broadcasted_iota(jnp.int32, (tm, tm), 0) < lax.broadcasted_iota(jnp.int32, (tm, tm), 1))
    rank = jnp.dot(jnp.where(member, 1.0, 0.0).astype(BF16), jnp.where(before, 1.0, 0.0).astype(BF16),
                   preferred_element_type=F32)
    local = jnp.zeros((n_experts, 1), jnp.int32)
    for e in range(n_experts):
        local = jnp.where(lax.broadcasted_iota(jnp.int32, (n_experts, 1), 0) == e,
                          local_ref[tile * n_experts + e], local)
    slot = rank + local.astype(F32)
    s1 = jnp.sum(jnp.where(expert == e1, slot, 0.0), axis=0, keepdims=True)
    s2 = jnp.sum(jnp.where(expert == e2, slot, 0.0), axis=0, keepdims=True)
    row = lax.broadcasted_iota(jnp.int32, (s_loc, tm), 0).astype(F32)
    place = jnp.where((row == s1) | (row == s2), 1.0, 0.0).astype(BF16)
    seg_ref[...] = jnp.dot(place, h_ref[...], preferred_element_type=F32).astype(BF16)

    def make_copy(local_row, global_row, rows):
        return pltpu.make_async_copy(seg_ref.at[pl.ds(local_row, rows)], o_ref.at[pl.ds(global_row, rows)], sem)

    _segment_copies(count_ref, local_ref, global_ref, tile, n_experts, make_copy, "start")
    _segment_copies(count_ref, local_ref, global_ref, tile, n_experts, make_copy, "wait")


def dispatch_tokens(h, e1_rows, e2_rows, plan, n_rows, n_experts, tm, s_loc):
    T, D = h.shape
    count, local, glob = plan
    grid_spec = pltpu.PrefetchScalarGridSpec(
        num_scalar_prefetch=3,
        grid=(T // tm,),
        in_specs=[pl.BlockSpec((1, 1, tm), lambda t, *_: (t, 0, 0)),
                  pl.BlockSpec((1, 1, tm), lambda t, *_: (t, 0, 0)),
                  pl.BlockSpec((tm, D), lambda t, *_: (t, 0)),
                  pl.BlockSpec(memory_space=pl.ANY)],
        out_specs=pl.BlockSpec(memory_space=pl.ANY),
        scratch_shapes=[pltpu.VMEM((s_loc, D), BF16), pltpu.SemaphoreType.DMA(())],
    )
    return pl.pallas_call(
        functools.partial(_dispatch_kernel, n_experts=n_experts),
        grid_spec=grid_spec,
        out_shape=jax.ShapeDtypeStruct((n_rows, D), BF16),
        input_output_aliases={6: 0},
        compiler_params=_cparams(1),
        name="dispatch_tokens",
    )(count, local, glob, e1_rows, e2_rows, h, jnp.zeros((n_rows, D), BF16))


def _expert_ffn_kernel(te_ref, valid_ref, h_ref, wg_ref, wu_ref, wd_ref, o_ref, acc_ref):
    i, j = pl.program_id(0), pl.program_id(1)
    last = pl.num_programs(1) - 1

    @pl.when(j == 0)
    def _():
        acc_ref[...] = jnp.zeros_like(acc_ref)

    @pl.when(valid_ref[i] > 0)
    def _():
        acc_ref[...] += _swiglu_chunk(h_ref[...], wg_ref[...], wu_ref[...], wd_ref[...])

    @pl.when(j == last)
    def _():
        o_ref[...] = acc_ref[...].astype(BF16)


def expert_ffn(h_sorted, tile_expert, tile_valid, wg, wu, wd, tm, tf):
    rows, D = h_sorted.shape
    F = wg.shape[2]
    grid_spec = pltpu.PrefetchScalarGridSpec(
        num_scalar_prefetch=2,
        grid=(rows // tm, F // tf),
        in_specs=[pl.BlockSpec((tm, D), lambda i, j, te, va: (i, 0)),
                  pl.BlockSpec((None, D, tf), lambda i, j, te, va: (te[i], 0, j)),
                  pl.BlockSpec((None, D, tf), lambda i, j, te, va: (te[i], 0, j)),
                  pl.BlockSpec((None, tf, D), lambda i, j, te, va: (te[i], j, 0))],
        out_specs=pl.BlockSpec((tm, D), lambda i, j, te, va: (i, 0)),
        scratch_shapes=[pltpu.VMEM((tm, D), F32)],
    )
    return pl.pallas_call(
        _expert_ffn_kernel,
        grid_spec=grid_spec,
        out_shape=jax.ShapeDtypeStruct((rows, D), BF16),
        compiler_params=_cparams(2),
        name="expert_ffn",
    )(tile_expert, tile_valid, h_sorted, wg, wu, wd)


def _combine_kernel(count_ref, local_ref, global_ref, y_ref, route_ref, x_ref, g2_ref, gain_ref, o_ref, buf_ref, sem,
                    *, n_experts):
    tile = pl.program_id(0)
    tm = o_ref.shape[0]
    s_loc = buf_ref.shape[0]
    buf_ref[...] = jnp.zeros_like(buf_ref)

    def make_copy(local_row, global_row, rows):
        return pltpu.make_async_copy(y_ref.at[pl.ds(global_row, rows)], buf_ref.at[pl.ds(local_row, rows)], sem)

    _segment_copies(count_ref, local_ref, global_ref, tile, n_experts, make_copy, "start")
    route = route_ref[...]
    e1 = route[:, 0:1].astype(jnp.int32)
    e2 = route[:, 1:2].astype(jnp.int32)
    expert = lax.broadcasted_iota(jnp.int32, (tm, HEAD_LANES), 1)
    member = (expert == e1) | (expert == e2)
    before = (lax.broadcasted_iota(jnp.int32, (tm, tm), 1) < lax.broadcasted_iota(jnp.int32, (tm, tm), 0))
    rank = jnp.dot(jnp.where(before, 1.0, 0.0).astype(BF16), jnp.where(member, 1.0, 0.0).astype(BF16),
                   preferred_element_type=F32)
    local = jnp.zeros((1, HEAD_LANES), jnp.int32)
    for e in range(n_experts):
        local = jnp.where(lax.broadcasted_iota(jnp.int32, (1, HEAD_LANES), 1) == e,
                          local_ref[tile * n_experts + e], local)
    slot = rank + local.astype(F32)
    s1 = jnp.sum(jnp.where(expert == e1, slot, 0.0), axis=1, keepdims=True)
    s2 = jnp.sum(jnp.where(expert == e2, slot, 0.0), axis=1, keepdims=True)
    col = lax.broadcasted_iota(jnp.int32, (tm, s_loc), 1).astype(F32)
    pick1 = jnp.where(col == s1, 1.0, 0.0).astype(BF16)
    pick2 = jnp.where(col == s2, 1.0, 0.0).astype(BF16)
    _segment_copies(count_ref, local_ref, global_ref, tile, n_experts, make_copy, "wait")
    y_loc = buf_ref[...]
    y = (route[:, 2:3] * jnp.dot(pick1, y_loc, preferred_element_type=F32)
         + route[:, 3:4] * jnp.dot(pick2, y_loc, preferred_element_type=F32))
    xn = x_ref[...] + g2_ref[0] * y
    o_ref[...] = _rms(xn) * gain_ref[...]


def combine_final(y_sorted, plan, route, x, mod, mod_base, rows_per_group, final_gain, n_experts, tm, s_loc):
    rows, D = x.shape
    count, local, glob = plan
    tiles_per_group = rows_per_group // tm
    grid_spec = pltpu.PrefetchScalarGridSpec(
        num_scalar_prefetch=3,
        grid=(rows // tm,),
        in_specs=[pl.BlockSpec(memory_space=pl.ANY),
                  pl.BlockSpec((tm, HEAD_LANES), lambda i, *_: (i, 0)),
                  pl.BlockSpec((tm, D), lambda i, *_: (i, 0)),
                  pl.BlockSpec((1, 1, D), lambda i, *_: (mod_base + (i // tiles_per_group) * N_MOD + 5, 0, 0)),
                  pl.BlockSpec((1, D), lambda i, *_: (0, 0))],
        out_specs=pl.BlockSpec((tm, D), lambda i, *_: (i, 0)),
        scratch_shapes=[pltpu.VMEM((s_loc, D), BF16), pltpu.SemaphoreType.DMA(())],
    )
    return pl.pallas_call(
        functools.partial(_combine_kernel, n_experts=n_experts),
        grid_spec=grid_spec,
        out_shape=jax.ShapeDtypeStruct((rows, D), F32),
        compiler_params=_cparams(1),
        name="combine_final",
    )(count, local, glob, y_sorted, route, x, mod, final_gain.reshape(1, D))


def routing_plan(route, n_experts, tm):
    T = route.shape[0]
    n_tok_tiles = T // tm
    e1 = route[:, 0].astype(jnp.int32)
    e2 = route[:, 1].astype(jnp.int32)
    experts = jnp.arange(n_experts)[None, :]
    member = ((e1[:, None] == experts) | (e2[:, None] == experts)).astype(jnp.int32)
    count = jnp.sum(member.reshape(n_tok_tiles, tm, n_experts), axis=1)
    count = ((count + SEG_ALIGN - 1) // SEG_ALIGN) * SEG_ALIGN
    local = jnp.cumsum(count, axis=1) - count
    total = jnp.sum(count, axis=0)
    padded = ((total + tm - 1) // tm) * tm
    ends = jnp.cumsum(padded)
    glob = (ends - padded)[None, :] + jnp.cumsum(count, axis=0) - count
    n_tiles = (TOP_K * T + n_tok_tiles * n_experts * (SEG_ALIGN - 1) + n_experts * (tm - 1)) // tm + 1
    tile_start = jnp.arange(n_tiles) * tm
    tile_valid = (tile_start < ends[-1]).astype(jnp.int32)
    tile_expert = jnp.sum((tile_start[:, None] >= ends[None, :]).astype(jnp.int32), axis=1)
    last_expert = jnp.max(jnp.where(tile_valid > 0, tile_expert, 0))
    tile_expert = jnp.where(tile_valid > 0, tile_expert, last_expert)
    plan = tuple(a.reshape(-1).astype(jnp.int32) for a in (count, local, glob))
    return plan, e1.reshape(n_tok_tiles, 1, tm), e2.reshape(n_tok_tiles, 1, tm), tile_expert, tile_valid, n_tiles * tm


def kernel(x, c, ctx, c_ctx, l0_w_ada, l0_b_ada, l0_norm_mix, l0_w_qkv, l0_rpb, l0_w_o, l0_norm_ffn, l0_w_gate, l0_w_up, l0_w_down, l1_w_ada, l1_b_ada, l1_norm_mix, l1_w_qkv, l1_lambda_q1, l1_lambda_k1, l1_lambda_q2, l1_lambda_k2, l1_subln, l1_w_o, l1_norm_ffn, l1_w_router, l1_w_gate, l1_w_up, l1_w_down, final_norm):
    B, S, D = x.shape
    C = ctx.shape[1]
    T, TC = B * S, B * C
    n_experts = l1_w_router.shape[1]
    tm = min(512, S)
    tmc = min(512, TC)

    xf = x.reshape(T, D)
    cf = ctx.reshape(TC, D)

    n_cond = ((B + 1 + 7) // 8) * 8
    cond = jnp.zeros((n_cond, D), F32).at[:B].set(c).at[B].set(c_ctx)
    ctx_base = B * N_MOD

    def qkv_weight(w, scale):
        col_scale = jnp.concatenate([jnp.full((D,), scale, F32), jnp.ones((2 * D,), F32)])
        return (w * col_scale[None, :]).astype(BF16)

    mod0 = ada_params(cond, l0_w_ada, l0_b_ada)
    w_qkv0 = qkv_weight(l0_w_qkv, (D // NA_HEADS) ** -0.5)
    qkv = norm_mod_qkv(xf, l0_norm_mix, mod0, 0, S, w_qkv0, tm)
    qkv_c = norm_mod_qkv(cf, l0_norm_mix, mod0, ctx_base, TC, w_qkv0, tmc)
    bias = na_bias(l0_rpb, S // GRID_W)
    o = neighbourhood_attention(qkv, qkv_c, bias, B, S, C, D)
    oc = ctx_attention(qkv_c, B, C, D)
    w_o0 = l0_w_o.astype(BF16)
    xf, h = oproj_residual_norm(o, w_o0, xf, l0_norm_ffn, mod0, 0, S, tm)
    cf, hc = oproj_residual_norm(oc, w_o0, cf, l0_norm_ffn, mod0, ctx_base, TC, tmc)
    wg0, wu0, wd0 = l0_w_gate.astype(BF16), l0_w_up.astype(BF16), l0_w_down.astype(BF16)
    d_ff = l0_w_gate.shape[1]
    tf0 = d_ff
    xf = ffn_residual(h, wg0, wu0, wd0, xf, mod0, 0, S, tm, tf0)
    cf = ffn_residual(hc, wg0, wu0, wd0, cf, mod0, ctx_base, TC, tmc, tf0)

    mod1 = ada_params(cond, l1_w_ada, l1_b_ada)
    w_qkv1 = qkv_weight(l1_w_qkv, DIFF_HEAD_DIM ** -0.5)
    qkv = norm_mod_qkv(xf, l1_norm_mix, mod1, 0, S, w_qkv1, tm, rope_tables=rope_tables(S))
    qkv_c = norm_mod_qkv(cf, l1_norm_mix, mod1, ctx_base, TC, w_qkv1, tmc)
    lam_init = 0.8 - 0.6 * math.exp(-0.3 * 1)
    lam_params = jnp.stack([l1_lambda_q1, l1_lambda_k1, l1_lambda_q2, l1_lambda_k2]).astype(F32)
    o = diff_attention(qkv, qkv_c, lam_params, l1_subln, lam_init, B, S, C, D, tq=min(1024, S))
    w_router = jnp.zeros((D, HEAD_LANES), F32).at[:, :n_experts].set(l1_w_router)
    w_router_hi = w_router.astype(BF16)
    w_router = jnp.concatenate([w_router_hi, (w_router - w_router_hi.astype(F32)).astype(BF16)], axis=1)
    xf, h, route = oproj_residual_norm(o, l1_w_o.astype(BF16), xf, l1_norm_ffn, mod1, 0, S, tm,
                                       w_router=w_router, n_experts=n_experts)
    tme = min(512, S)
    s_loc = -(-(TOP_K * tme + n_experts * (SEG_ALIGN - 1)) // HEAD_LANES) * HEAD_LANES
    plan, e1_rows, e2_rows, tile_expert, tile_valid, n_rows = routing_plan(route, n_experts, tme)
    h_sorted = dispatch_tokens(h, e1_rows, e2_rows, plan, n_rows, n_experts, tme, s_loc)
    d_ffe = l1_w_gate.shape[2]
    tfe = d_ffe // 2 if (d_ffe // 2) % 128 == 0 else d_ffe
    y_sorted = expert_ffn(h_sorted, tile_expert, tile_valid, l1_w_gate.astype(BF16), l1_w_up.astype(BF16),
                          l1_w_down.astype(BF16), tme, tfe)
    out = combine_final(y_sorted, plan, route, xf, mod1, 0, S, final_norm, n_experts, tme, s_loc)
    return out.reshape(B, S, D)
```

```python
import functools
import math

import jax
import jax.numpy as jnp
from jax import lax
from jax.experimental import pallas as pl
from jax.experimental.pallas import tpu as pltpu

F32 = jnp.float32
BF16 = jnp.bfloat16

GRID_W = 64
NA_HEADS = 16
NA_WIN_R = 8
NA_WIN_C = 16
NA_QROWS = 8
NA_KROWS = 16
NA_STEP_BLOCKS = 2
DIFF_HEADS = 8
DIFF_HEAD_DIM = 64
ROPE_THETA = 10000.0
TOP_K = 2
NORM_EPS = 1e-6
HEAD_LANES = 128
MASK_VALUE = -1e30
N_MOD = 6
VMEM_LIMIT = 56 * 1024 * 1024


def _cparams(n_axes, vmem=VMEM_LIMIT):
    return pltpu.CompilerParams(dimension_semantics=("arbitrary",) * n_axes, vmem_limit_bytes=vmem)


def _rms(x):
    return x * lax.rsqrt(jnp.mean(x * x, axis=-1, keepdims=True) + NORM_EPS)


def _ada_kernel(c_ref, w_ref, b_ref, o_ref):
    c = c_ref[...]
    s = c * jax.nn.sigmoid(c)
    o_ref[...] = jnp.dot(s, w_ref[...], preferred_element_type=F32,
                         precision=lax.Precision.HIGHEST) + b_ref[...]


def ada_params(cond, w_ada, b_ada):
    R, D = cond.shape
    N = w_ada.shape[1]
    tn = N // 4
    out = pl.pallas_call(
        _ada_kernel,
        grid=(N // tn,),
        in_specs=[pl.BlockSpec((R, D), lambda j: (0, 0)),
                  pl.BlockSpec((D, tn), lambda j: (0, j)),
                  pl.BlockSpec((1, tn), lambda j: (0, j))],
        out_specs=pl.BlockSpec((R, tn), lambda j: (0, j)),
        out_shape=jax.ShapeDtypeStruct((R, N), F32),
        compiler_params=_cparams(1),
        name="ada_params",
    )(cond, w_ada, b_ada.reshape(1, N))
    return out.reshape(R * N_MOD, 1, D)


def _qkv_kernel(*refs, rope, d_model):
    if rope:
        x_ref, g_ref, sh_ref, sc_ref, w_ref, cos_ref, sa_ref, sb_ref, o_ref = refs
    else:
        x_ref, g_ref, sh_ref, sc_ref, w_ref, o_ref = refs
    h = (_rms(x_ref[...]) * g_ref[...]) * (1.0 + sc_ref[0]) + sh_ref[0]
    hb = h.astype(BF16)
    n_chunks = w_ref.shape[1] // d_model
    for n in range(n_chunks):
        acc = jnp.dot(hb, w_ref[:, n * d_model:(n + 1) * d_model], preferred_element_type=F32)
        if rope and n < 2:
            cos, sa, sb = cos_ref[...], sa_ref[...], sb_ref[...]
            for s in range(d_model // HEAD_LANES):
                xs = acc[:, s * HEAD_LANES:(s + 1) * HEAD_LANES]
                rot = xs * cos + pltpu.roll(xs, HEAD_LANES - 16, 1) * sa + pltpu.roll(xs, 16, 1) * sb
                col = n * d_model + s * HEAD_LANES
                o_ref[:, col:col + HEAD_LANES] = rot.astype(BF16)
        else:
            o_ref[:, n * d_model:(n + 1) * d_model] = acc.astype(BF16)


def norm_mod_qkv(x, gain, mod, mod_base, rows_per_group, w_bf16, tm, rope_tables=None):
    rows, D = x.shape
    N = w_bf16.shape[1]
    tiles_per_group = rows_per_group // tm
    in_specs = [pl.BlockSpec((tm, D), lambda i: (i, 0)),
                pl.BlockSpec((1, D), lambda i: (0, 0)),
                pl.BlockSpec((1, 1, D), lambda i: (mod_base + (i // tiles_per_group) * N_MOD + 0, 0, 0)),
                pl.BlockSpec((1, 1, D), lambda i: (mod_base + (i // tiles_per_group) * N_MOD + 1, 0, 0)),
                pl.BlockSpec((D, N), lambda i: (0, 0))]
    args = [x, gain.reshape(1, D), mod, mod, w_bf16]
    if rope_tables is not None:
        seq_tiles = rope_tables[0].shape[0] // tm
        for t in rope_tables:
            in_specs.append(pl.BlockSpec((tm, HEAD_LANES), lambda i: (i % seq_tiles, 0)))
            args.append(t)
    return pl.pallas_call(
        functools.partial(_qkv_kernel, rope=rope_tables is not None, d_model=D),
        grid=(rows // tm,),
        in_specs=in_specs,
        out_specs=pl.BlockSpec((tm, N), lambda i: (i, 0)),
        out_shape=jax.ShapeDtypeStruct((rows, N), BF16),
        compiler_params=_cparams(1),
        name="norm_mod_qkv",
    )(*args)


def rope_tables(seq):
    half = DIFF_HEAD_DIM // 2
    n_freq = half // 2
    inv_freq = ROPE_THETA ** (-jnp.arange(0, half, 2, dtype=F32) / half)
    pos = jnp.arange(seq)
    rows_pos = (pos // GRID_W).astype(F32)
    cols_pos = (pos % GRID_W).astype(F32)
    lane = jnp.arange(HEAD_LANES)
    d = lane % DIFF_HEAD_DIM
    use_col = (d // half) == 1
    j = d % half
    freq = inv_freq[j % n_freq]
    ang = jnp.where(use_col[None, :], cols_pos[:, None], rows_pos[:, None]) * freq[None, :]
    cos, sin = jnp.cos(ang), jnp.sin(ang)
    first = (j < n_freq)[None, :]
    return cos, jnp.where(first, -sin, 0.0), jnp.where(first, 0.0, sin)


def _head_mask(q, head):
    lane = lax.broadcasted_iota(jnp.int32, (1, HEAD_LANES), 1)
    return jnp.where((lane // 64) == head, q, jnp.zeros_like(q))


def _qk(q, k):
    return lax.dot_general(q, k, (((1,), (1,)), ((), ())), preferred_element_type=F32)


def _softmax2(s_a, s_b):
    m = jnp.maximum(jnp.max(s_a, axis=-1, keepdims=True), jnp.max(s_b, axis=-1, keepdims=True))
    p_a = jnp.exp(s_a - m)
    p_b = jnp.exp(s_b - m)
    denom = jnp.sum(p_a, axis=-1, keepdims=True) + jnp.sum(p_b, axis=-1, keepdims=True)
    return p_a, p_b, 1.0 / denom


def _na_kernel(q_ref, k_ref, v_ref, kc_ref, vc_ref, bias_ref, o_ref, *, grid_rows):
    g = pl.program_id(0)
    k_row0 = jnp.clip(g * NA_QROWS - NA_WIN_R // 2, 0, grid_rows - NA_KROWS) * GRID_W
    k_row0 = pl.multiple_of(k_row0, GRID_W)
    n_keys = NA_KROWS * GRID_W
    lane = lax.broadcasted_iota(jnp.int32, (1, HEAD_LANES), 1)
    for blk in range(q_ref.shape[1] // HEAD_LANES):
        cols = slice(blk * HEAD_LANES, (blk + 1) * HEAD_LANES)
        q = q_ref[:, cols]
        k_lat = k_ref[pl.ds(k_row0, n_keys), cols]
        v_lat = v_ref[pl.ds(k_row0, n_keys), cols]
        kc, vc = kc_ref[:, cols], vc_ref[:, cols]
        outs = []
        for head in range(2):
            qm = _head_mask(q, head)
            s_lat = _qk(qm, k_lat) + bias_ref[0, 2 * blk + head]
            s_ctx = _qk(qm, kc)
            p_lat, p_ctx, inv = _softmax2(s_lat, s_ctx)
            o = (jnp.dot(p_lat.astype(BF16), v_lat, preferred_element_type=F32)
                 + jnp.dot(p_ctx.astype(BF16), vc, preferred_element_type=F32))
            outs.append(o * inv)
        o_ref[:, cols] = jnp.where(lane < 64, outs[0], outs[1]).astype(BF16)


NA_ROW_OFFSETS = 2 * NA_WIN_R - 1
NA_PAIR_ENTRIES = NA_ROW_OFFSETS + 3


def na_pair_table(rpb):
    H = rpb.shape[0]
    qc = jnp.arange(GRID_W)[:, None]
    kcol = jnp.arange(GRID_W)[None, :]
    ws = jnp.clip(qc - NA_WIN_C // 2, 0, GRID_W - NA_WIN_C)
    col_ok = (kcol >= ws) & (kcol < ws + NA_WIN_C)
    dc = jnp.clip(kcol - qc, -(NA_WIN_C - 1), NA_WIN_C - 1) + (NA_WIN_C - 1)
    blocks = jnp.where(col_ok[None, None], rpb[:, :, dc], MASK_VALUE)
    pad = jnp.full((H, 2, GRID_W, GRID_W), MASK_VALUE, F32)
    padded = jnp.concatenate([pad, blocks, pad], axis=1)
    return jnp.concatenate([padded[:, :-1], padded[:, 1:]], axis=-1)


def _na_bias_kernel(tp_ref, o_ref, *, grid_rows):
    g = pl.program_id(0)
    k_row0 = jnp.clip(g * NA_QROWS - NA_WIN_R // 2, 0, grid_rows - NA_KROWS)
    lane = lax.broadcasted_iota(jnp.int32, (1, HEAD_LANES), 1)
    for qr in range(NA_QROWS):
        r = g * NA_QROWS + qr
        rs = jnp.clip(r - NA_WIN_R // 2, 0, grid_rows - NA_WIN_R)
        for pair in range(NA_KROWS // 2):
            kr = k_row0 + 2 * pair
            ok0 = ((kr >= rs) & (kr < rs + NA_WIN_R)).astype(jnp.int32)
            ok1 = ((kr + 1 >= rs) & (kr + 1 < rs + NA_WIN_R)).astype(jnp.int32)
            entry = jnp.clip(kr - r + (NA_WIN_R - 1) + 2, 0, NA_PAIR_ENTRIES - 1)
            ok = jnp.where(lane < GRID_W, ok0, ok1) > 0
            o_ref[0, 0, qr * GRID_W:(qr + 1) * GRID_W, pair * HEAD_LANES:(pair + 1) * HEAD_LANES] = (
                jnp.where(ok, tp_ref[0, entry], MASK_VALUE))


def na_bias(rpb, grid_rows):
    H = rpb.shape[0]
    n_groups = grid_rows // NA_QROWS
    tq, nk = NA_QROWS * GRID_W, NA_KROWS * GRID_W
    return pl.pallas_call(
        functools.partial(_na_bias_kernel, grid_rows=grid_rows),
        grid=(n_groups, H),
        in_specs=[pl.BlockSpec((1, NA_PAIR_ENTRIES, GRID_W, HEAD_LANES), lambda g, h: (h, 0, 0, 0))],
        out_specs=pl.BlockSpec((1, 1, tq, nk), lambda g, h: (g, h, 0, 0)),
        out_shape=jax.ShapeDtypeStruct((n_groups, H, tq, nk), F32),
        compiler_params=_cparams(2),
        name="na_bias",
    )(na_pair_table(rpb))


def neighbourhood_attention(qkv, qkv_ctx, bias, batch, seq, ctx_len, d_model):
    grid_rows = seq // GRID_W
    n_groups = grid_rows // NA_QROWS
    tq = NA_QROWS * GRID_W
    width = NA_STEP_BLOCKS * HEAD_LANES
    n_hp = d_model // width
    nk = NA_KROWS * GRID_W
    return pl.pallas_call(
        functools.partial(_na_kernel, grid_rows=grid_rows),
        grid=(n_groups, n_hp, batch),
        in_specs=[pl.BlockSpec((tq, width), lambda g, h, b: (b * n_groups + g, h)),
                  pl.BlockSpec((seq, width), lambda g, h, b: (b, n_hp + h)),
                  pl.BlockSpec((seq, width), lambda g, h, b: (b, 2 * n_hp + h)),
                  pl.BlockSpec((ctx_len, width), lambda g, h, b: (b, n_hp + h)),
                  pl.BlockSpec((ctx_len, width), lambda g, h, b: (b, 2 * n_hp + h)),
                  pl.BlockSpec((1, 2 * NA_STEP_BLOCKS, tq, nk), lambda g, h, b: (g, h, 0, 0))],
        out_specs=pl.BlockSpec((tq, width), lambda g, h, b: (b * n_groups + g, h)),
        out_shape=jax.ShapeDtypeStruct((batch * seq, d_model), BF16),
        compiler_params=_cparams(3),
        name="neighbourhood_attention",
    )(qkv, qkv, qkv, qkv_ctx, qkv_ctx, bias)


def _ctx_attn_kernel(q_ref, k_ref, v_ref, o_ref):
    lane = lax.broadcasted_iota(jnp.int32, (1, HEAD_LANES), 1)
    for blk in range(q_ref.shape[1] // HEAD_LANES):
        cols = slice(blk * HEAD_LANES, (blk + 1) * HEAD_LANES)
        q, k, v = q_ref[:, cols], k_ref[:, cols], v_ref[:, cols]
        outs = []
        for head in range(2):
            s = _qk(_head_mask(q, head), k)
            m = jnp.max(s, axis=-1, keepdims=True)
            p = jnp.exp(s - m)
            inv = 1.0 / jnp.sum(p, axis=-1, keepdims=True)
            outs.append(jnp.dot(p.astype(BF16), v, preferred_element_type=F32) * inv)
        o_ref[:, cols] = jnp.where(lane < 64, outs[0], outs[1]).astype(BF16)


def ctx_attention(qkv_ctx, batch, ctx_len, d_model):
    return pl.pallas_call(
        _ctx_attn_kernel,
        grid=(batch,),
        in_specs=[pl.BlockSpec((ctx_len, d_model), lambda b: (b, 0)),
                  pl.BlockSpec((ctx_len, d_model), lambda b: (b, 1)),
                  pl.BlockSpec((ctx_len, d_model), lambda b: (b, 2))],
        out_specs=pl.BlockSpec((ctx_len, d_model), lambda b: (b, 0)),
        out_shape=jax.ShapeDtypeStruct((batch * ctx_len, d_model), BF16),
        compiler_params=_cparams(1),
        name="ctx_attention",
    )(qkv_ctx, qkv_ctx, qkv_ctx)


def _diff_attn_kernel(q_ref, k_ref, v_ref, kc_ref, vc_ref, lam_ref, subln_ref, o_ref, *, lam_init, sub_rows):
    lp = lam_ref[...]
    lam = (jnp.exp(jnp.sum(lp[0:1] * lp[1:2], axis=-1, keepdims=True))
           - jnp.exp(jnp.sum(lp[2:3] * lp[3:4], axis=-1, keepdims=True)) + lam_init)
    k, v, kc, vc = k_ref[...], v_ref[...], kc_ref[...], vc_ref[...]

    def attend(qm):
        p_lat, p_ctx, inv = _softmax2(_qk(qm, k), _qk(qm, kc))
        return (jnp.dot(p_lat.astype(BF16), v, preferred_element_type=F32)
                + jnp.dot(p_ctx.astype(BF16), vc, preferred_element_type=F32)), inv

    for r0 in range(0, q_ref.shape[0], sub_rows):
        q = q_ref[r0:r0 + sub_rows, :]
        o1, inv1 = attend(_head_mask(q, 0))
        o2, inv2 = attend(_head_mask(q, 1))
        o = o1 * inv1 - o2 * (lam * inv2)
        o_ref[r0:r0 + sub_rows, :] = ((_rms(o) * subln_ref[...]) * (1.0 - lam_init)).astype(BF16)


def diff_attention(qkv, qkv_ctx, lam_params, subln, lam_init, batch, seq, ctx_len, d_model, tq, sub_rows=256):
    n_h = d_model // HEAD_LANES
    nq = seq // tq
    return pl.pallas_call(
        functools.partial(_diff_attn_kernel, lam_init=lam_init, sub_rows=min(sub_rows, tq)),
        grid=(batch, n_h, nq),
        in_specs=[pl.BlockSpec((tq, HEAD_LANES), lambda b, h, i: (b * nq + i, h)),
                  pl.BlockSpec((seq, HEAD_LANES), lambda b, h, i: (b, n_h + h)),
                  pl.BlockSpec((seq, HEAD_LANES), lambda b, h, i: (b, 2 * n_h + h)),
                  pl.BlockSpec((ctx_len, HEAD_LANES), lambda b, h, i: (b, n_h + h)),
                  pl.BlockSpec((ctx_len, HEAD_LANES), lambda b, h, i: (b, 2 * n_h + h)),
                  pl.BlockSpec(lam_params.shape, lambda b, h, i: (0, 0)),
                  pl.BlockSpec((1, HEAD_LANES), lambda b, h, i: (0, 0))],
        out_specs=pl.BlockSpec((tq, HEAD_LANES), lambda b, h, i: (b * nq + i, h)),
        out_shape=jax.ShapeDtypeStruct((batch * seq, d_model), BF16),
        compiler_params=_cparams(3),
        name="diff_attention",
    )(qkv, qkv, qkv, qkv_ctx, qkv_ctx, lam_params, subln.reshape(1, HEAD_LANES))


def _oproj_kernel(*refs, route, n_experts):
    if route:
        o_ref, w_ref, x_ref, g1_ref, gain_ref, sh_ref, sc_ref, wr_ref, xo_ref, h_ref, r_ref = refs
    else:
        o_ref, w_ref, x_ref, g1_ref, gain_ref, sh_ref, sc_ref, xo_ref, h_ref = refs
    y = jnp.dot(o_ref[...], w_ref[...], preferred_element_type=F32)
    xn = x_ref[...] + g1_ref[0] * y
    xo_ref[...] = xn
    h = (_rms(xn) * gain_ref[...]) * (1.0 + sc_ref[0]) + sh_ref[0]
    h_hi = h.astype(BF16)
    h_ref[...] = h_hi
    if not route:
        return
    n_exp = n_experts
    h_lo = (h - h_hi.astype(F32)).astype(BF16)
    both = jnp.dot(h_hi, wr_ref[...], preferred_element_type=F32)
    logits = both[:, :HEAD_LANES] + (jnp.dot(h_lo, wr_ref[:, :HEAD_LANES], preferred_element_type=F32)
                                     + both[:, HEAD_LANES:])
    lane = lax.broadcasted_iota(jnp.int32, logits.shape, 1)
    logits = jnp.where(lane < n_exp, logits, -jnp.inf)
    v1 = jnp.max(logits, axis=-1, keepdims=True)
    i1 = jnp.min(jnp.where(logits == v1, lane, n_exp), axis=-1, keepdims=True)
    rest = jnp.where(lane == i1, -jnp.inf, logits)
    v2 = jnp.max(rest, axis=-1, keepdims=True)
    i2 = jnp.min(jnp.where(rest == v2, lane, n_exp), axis=-1, keepdims=True)
    e = jnp.exp(v2 - v1)
    w1 = 1.0 / (1.0 + e)
    w2 = e * w1
    out_lane = lax.broadcasted_iota(jnp.int32, r_ref.shape, 1)
    r_ref[...] = jnp.where(out_lane == 0, i1.astype(F32),
                           jnp.where(out_lane == 1, i2.astype(F32),
                                     jnp.where(out_lane == 2, w1, jnp.where(out_lane == 3, w2, 0.0))))


def oproj_residual_norm(o, w_o_bf16, x, gain, mod, mod_base, rows_per_group, tm, w_router=None, n_experts=0):
    rows, D = x.shape
    tiles_per_group = rows_per_group // tm

    def mod_spec(k):
        return pl.BlockSpec((1, 1, D), lambda i: (mod_base + (i // tiles_per_group) * N_MOD + k, 0, 0))

    in_specs = [pl.BlockSpec((tm, D), lambda i: (i, 0)),
                pl.BlockSpec((D, D), lambda i: (0, 0)),
                pl.BlockSpec((tm, D), lambda i: (i, 0)),
                mod_spec(2),
                pl.BlockSpec((1, D), lambda i: (0, 0)),
                mod_spec(3), mod_spec(4)]
    args = [o, w_o_bf16, x, mod, gain.reshape(1, D), mod, mod]
    out_specs = [pl.BlockSpec((tm, D), lambda i: (i, 0)), pl.BlockSpec((tm, D), lambda i: (i, 0))]
    out_shape = [jax.ShapeDtypeStruct((rows, D), F32),
                 jax.ShapeDtypeStruct((rows, D), BF16)]
    if w_router is not None:
        in_specs.append(pl.BlockSpec(w_router.shape, lambda i: (0, 0)))
        args.append(w_router)
        out_specs.append(pl.BlockSpec((tm, HEAD_LANES), lambda i: (i, 0)))
        out_shape.append(jax.ShapeDtypeStruct((rows, HEAD_LANES), F32))
    return pl.pallas_call(
        functools.partial(_oproj_kernel, route=w_router is not None, n_experts=n_experts),
        grid=(rows // tm,),
        in_specs=in_specs,
        out_specs=out_specs,
        out_shape=out_shape,
        compiler_params=_cparams(1),
        name="oproj_residual_norm",
    )(*args)


def _swiglu_chunk(h, wg, wu, wd):
    a = jnp.dot(h, wg, preferred_element_type=F32)
    b = jnp.dot(h, wu, preferred_element_type=F32)
    t = (a * jax.nn.sigmoid(a)) * b
    return jnp.dot(t.astype(BF16), wd, preferred_element_type=F32)


def _ffn_kernel(h_ref, wg_ref, wu_ref, wd_ref, x_ref, g2_ref, o_ref, acc_ref):
    j = pl.program_id(1)

    @pl.when(j == 0)
    def _():
        acc_ref[...] = jnp.zeros_like(acc_ref)

    acc_ref[...] += _swiglu_chunk(h_ref[...], wg_ref[...], wu_ref[...], wd_ref[...])

    @pl.when(j == pl.num_programs(1) - 1)
    def _():
        o_ref[...] = x_ref[...] + g2_ref[0] * acc_ref[...]


def ffn_residual(h, wg, wu, wd, x, mod, mod_base, rows_per_group, tm, tf):
    rows, D = x.shape
    F = wg.shape[1]
    tiles_per_group = rows_per_group // tm
    w_mode = dict(pipeline_mode=pl.Buffered(1)) if tf == F else {}
    return pl.pallas_call(
        _ffn_kernel,
        grid=(rows // tm, F // tf),
        in_specs=[pl.BlockSpec((tm, D), lambda i, j: (i, 0)),
                  pl.BlockSpec((D, tf), lambda i, j: (0, j), **w_mode),
                  pl.BlockSpec((D, tf), lambda i, j: (0, j), **w_mode),
                  pl.BlockSpec((tf, D), lambda i, j: (j, 0), **w_mode),
                  pl.BlockSpec((tm, D), lambda i, j: (i, 0)),
                  pl.BlockSpec((1, 1, D), lambda i, j: (mod_base + (i // tiles_per_group) * N_MOD + 5, 0, 0))],
        out_specs=pl.BlockSpec((tm, D), lambda i, j: (i, 0)),
        out_shape=jax.ShapeDtypeStruct((rows, D), F32),
        scratch_shapes=[pltpu.VMEM((tm, D), F32)],
        compiler_params=_cparams(2),
        name="ffn_residual",
    )(h, wg, wu, wd, x, mod)


SEG_ALIGN = 16
SEG_DMA_ROWS = (512, 256, 128, 64, 32, 16)


def _segment_copies(count_ref, local_ref, global_ref, tile, n_experts, make_copy, action):
    for e in range(n_experts):
        idx = tile * n_experts + e
        count, local, glob = count_ref[idx], local_ref[idx], global_ref[idx]
        for rows in SEG_DMA_ROWS:
            done = count & ~(2 * rows - 1)

            @pl.when((count & rows) != 0)
            def _(rows=rows, done=done, local=local, glob=glob):
                copy = make_copy(pl.multiple_of(local + done, SEG_ALIGN), pl.multiple_of(glob + done, SEG_ALIGN), rows)
                getattr(copy, action)()


def _dispatch_kernel(count_ref, local_ref, global_ref, e1_ref, e2_ref, h_ref, init_ref, o_ref, seg_ref, sem,
                     *, n_experts):
    del init_ref
    tile = pl.program_id(0)
    tm = h_ref.shape[0]
    s_loc = seg_ref.shape[0]
    e1, e2 = e1_ref[0], e2_ref[0]
    expert = lax.broadcasted_iota(jnp.int32, (n_experts, tm), 0)
    member = (expert == e1) | (expert == e2)
    before = (lax.broadcasted_iota(jnp.int32, (tm, tm), 0) < lax.broadcasted_iota(jnp.int32, (tm, tm), 1))
    rank = jnp.dot(jnp.where(member, 1.0, 0.0).astype(BF16), jnp.where(before, 1.0, 0.0).astype(BF16),
                   preferred_element_type=F32)
    local = jnp.zeros((n_experts, 1), jnp.int32)
    for e in range(n_experts):
        local = jnp.where(lax.broadcasted_iota(jnp.int32, (n_experts, 1), 0) == e,
                          local_ref[tile * n_experts + e], local)
    slot = rank + local.astype(F32)
    s1 = jnp.sum(jnp.where(expert == e1, slot, 0.0), axis=0, keepdims=True)
    s2 = jnp.sum(jnp.where(expert == e2, slot, 0.0), axis=0, keepdims=True)
    row = lax.broadcasted_iota(jnp.int32, (s_loc, tm), 0).astype(F32)
    place = jnp.where((row == s1) | (row == s2), 1.0, 0.0).astype(BF16)
    seg_ref[...] = jnp.dot(place, h_ref[...], preferred_element_type=F32).astype(BF16)

    def make_copy(local_row, global_row, rows):
        return pltpu.make_async_copy(seg_ref.at[pl.ds(local_row, rows)], o_ref.at[pl.ds(global_row, rows)], sem)

    _segment_copies(count_ref, local_ref, global_ref, tile, n_experts, make_copy, "start")
    _segment_copies(count_ref, local_ref, global_ref, tile, n_experts, make_copy, "wait")


def dispatch_tokens(h, e1_rows, e2_rows, plan, n_rows, n_experts, tm, s_loc):
    T, D = h.shape
    count, local, glob = plan
    grid_spec = pltpu.PrefetchScalarGridSpec(
        num_scalar_prefetch=3,
        grid=(T // tm,),
        in_specs=[pl.BlockSpec((1, 1, tm), lambda t, *_: (t, 0, 0)),
                  pl.BlockSpec((1, 1, tm), lambda t, *_: (t, 0, 0)),
                  pl.BlockSpec((tm, D), lambda t, *_: (t, 0)),
                  pl.BlockSpec(memory_space=pl.ANY)],
        out_specs=pl.BlockSpec(memory_space=pl.ANY),
        scratch_shapes=[pltpu.VMEM((s_loc, D), BF16), pltpu.SemaphoreType.DMA(())],
    )
    return pl.pallas_call(
        functools.partial(_dispatch_kernel, n_experts=n_experts),
        grid_spec=grid_spec,
        out_shape=jax.ShapeDtypeStruct((n_rows, D), BF16),
        input_output_aliases={6: 0},
        compiler_params=_cparams(1),
        name="dispatch_tokens",
    )(count, local, glob, e1_rows, e2_rows, h, jnp.zeros((n_rows, D), BF16))


def _expert_ffn_kernel(te_ref, valid_ref, h_ref, wg_ref, wu_ref, wd_ref, o_ref, acc_ref):
    i, j = pl.program_id(0), pl.program_id(1)
    last = pl.num_programs(1) - 1

    @pl.when(j == 0)
    def _():
        acc_ref[...] = jnp.zeros_like(acc_ref)

    @pl.when(valid_ref[i] > 0)
    def _():
        acc_ref[...] += _swiglu_chunk(h_ref[...], wg_ref[...], wu_ref[...], wd_ref[...])

    @pl.when(j == last)
    def _():
        o_ref[...] = acc_ref[...].astype(BF16)


def expert_ffn(h_sorted, tile_expert, tile_valid, wg, wu, wd, tm, tf):
    rows, D = h_sorted.shape
    F = wg.shape[2]
    grid_spec = pltpu.PrefetchScalarGridSpec(
        num_scalar_prefetch=2,
        grid=(rows // tm, F // tf),
        in_specs=[pl.BlockSpec((tm, D), lambda i, j, te, va: (i, 0)),
                  pl.BlockSpec((None, D, tf), lambda i, j, te, va: (te[i], 0, j)),
                  pl.BlockSpec((None, D, tf), lambda i, j, te, va: (te[i], 0, j)),
                  pl.BlockSpec((None, tf, D), lambda i, j, te, va: (te[i], j, 0))],
        out_specs=pl.BlockSpec((tm, D), lambda i, j, te, va: (i, 0)),
        scratch_shapes=[pltpu.VMEM((tm, D), F32)],
    )
    return pl.pallas_call(
        _expert_ffn_kernel,
        grid_spec=grid_spec,
        out_shape=jax.ShapeDtypeStruct((rows, D), BF16),
        compiler_params=_cparams(2),
        name="expert_ffn",
    )(tile_expert, tile_valid, h_sorted, wg, wu, wd)


def _combine_kernel(count_ref, local_ref, global_ref, y_ref, route_ref, x_ref, g2_ref, gain_ref, o_ref, buf_ref, sem,
                    *, n_experts):
    tile = pl.program_id(0)
    tm = o_ref.shape[0]
    s_loc = buf_ref.shape[1]
    ring = tile % 2

    def copies(which_tile, which_slot, action):
        def make_copy(local_row, global_row, rows):
            return pltpu.make_async_copy(y_ref.at[pl.ds(global_row, rows)],
                                         buf_ref.at[which_slot, pl.ds(local_row, rows)], sem.at[which_slot])
        _segment_copies(count_ref, local_ref, global_ref, which_tile, n_experts, make_copy, action)

    def fetch(which_tile, which_slot):
        buf_ref.at[which_slot][...] = jnp.zeros((s_loc, buf_ref.shape[2]), buf_ref.dtype)
        copies(which_tile, which_slot, "start")

    @pl.when(tile == 0)
    def _():
        fetch(tile, ring)

    @pl.when(tile + 1 < pl.num_programs(0))
    def _():
        fetch(tile + 1, 1 - ring)

    route = route_ref[...]
    e1 = route[:, 0:1].astype(jnp.int32)
    e2 = route[:, 1:2].astype(jnp.int32)
    expert = lax.broadcasted_iota(jnp.int32, (tm, HEAD_LANES), 1)
    member = (expert == e1) | (expert == e2)
    before = (lax.broadcasted_iota(jnp.int32, (tm, tm), 1) < lax.broadcasted_iota(jnp.int32, (tm, tm), 0))
    rank = jnp.dot(jnp.where(before, 1.0, 0.0).astype(BF16), jnp.where(member, 1.0, 0.0).astype(BF16),
                   preferred_element_type=F32)
    local = jnp.zeros((1, HEAD_LANES), jnp.int32)
    for e in range(n_experts):
        local = jnp.where(lax.broadcasted_iota(jnp.int32, (1, HEAD_LANES), 1) == e,
                          local_ref[tile * n_experts + e], local)
    slot = rank + local.astype(F32)
    s1 = jnp.sum(jnp.where(expert == e1, slot, 0.0), axis=1, keepdims=True)
    s2 = jnp.sum(jnp.where(expert == e2, slot, 0.0), axis=1, keepdims=True)
    col = lax.broadcasted_iota(jnp.int32, (tm, s_loc), 1).astype(F32)
    pick1 = jnp.where(col == s1, 1.0, 0.0).astype(BF16)
    pick2 = jnp.where(col == s2, 1.0, 0.0).astype(BF16)
    copies(tile, ring, "wait")
    y_loc = buf_ref.at[ring][...]
    y = (route[:, 2:3] * jnp.dot(pick1, y_loc, preferred_element_type=F32)
         + route[:, 3:4] * jnp.dot(pick2, y_loc, preferred_element_type=F32))
    xn = x_ref[...] + g2_ref[0] * y
    o_ref[...] = _rms(xn) * gain_ref[...]


def combine_final(y_sorted, plan, route, x, mod, mod_base, rows_per_group, final_gain, n_experts, tm, s_loc):
    rows, D = x.shape
    count, local, glob = plan
    tiles_per_group = rows_per_group // tm
    grid_spec = pltpu.PrefetchScalarGridSpec(
        num_scalar_prefetch=3,
        grid=(rows // tm,),
        in_specs=[pl.BlockSpec(memory_space=pl.ANY),
                  pl.BlockSpec((tm, HEAD_LANES), lambda i, *_: (i, 0)),
                  pl.BlockSpec((tm, D), lambda i, *_: (i, 0)),
                  pl.BlockSpec((1, 1, D), lambda i, *_: (mod_base + (i // tiles_per_group) * N_MOD + 5, 0, 0)),
                  pl.BlockSpec((1, D), lambda i, *_: (0, 0))],
        out_specs=pl.BlockSpec((tm, D), lambda i, *_: (i, 0)),
        scratch_shapes=[pltpu.VMEM((2, s_loc, D), BF16), pltpu.SemaphoreType.DMA((2,))],
    )
    return pl.pallas_call(
        functools.partial(_combine_kernel, n_experts=n_experts),
        grid_spec=grid_spec,
        out_shape=jax.ShapeDtypeStruct((rows, D), F32),
        compiler_params=_cparams(1),
        name="combine_final",
    )(count, local, glob, y_sorted, route, x, mod, final_gain.reshape(1, D))


def routing_plan(route, n_experts, tm):
    T = route.shape[0]
    n_tok_tiles = T // tm
    e1 = route[:, 0].astype(jnp.int32)
    e2 = route[:, 1].astype(jnp.int32)
    experts = jnp.arange(n_experts)[None, :]
    member = ((e1[:, None] == experts) | (e2[:, None] == experts)).astype(jnp.int32)
    count = jnp.sum(member.reshape(n_tok_tiles, tm, n_experts), axis=1)
    count = ((count + SEG_ALIGN - 1) // SEG_ALIGN) * SEG_ALIGN
    local = jnp.cumsum(count, axis=1) - count
    total = jnp.sum(count, axis=0)
    padded = ((total + tm - 1) // tm) * tm
    ends = jnp.cumsum(padded)
    glob = (ends - padded)[None, :] + jnp.cumsum(count, axis=0) - count
    n_tiles = (TOP_K * T + n_tok_tiles * n_experts * (SEG_ALIGN - 1) + n_experts * (tm - 1)) // tm + 1
    tile_start = jnp.arange(n_tiles) * tm
    tile_valid = (tile_start < ends[-1]).astype(jnp.int32)
    tile_expert = jnp.sum((tile_start[:, None] >= ends[None, :]).astype(jnp.int32), axis=1)
    last_expert = jnp.max(jnp.where(tile_valid > 0, tile_expert, 0))
    tile_expert = jnp.where(tile_valid > 0, tile_expert, last_expert)
    plan = tuple(a.reshape(-1).astype(jnp.int32) for a in (count, local, glob))
    return plan, e1.reshape(n_tok_tiles, 1, tm), e2.reshape(n_tok_tiles, 1, tm), tile_expert, tile_valid, n_tiles * tm


def kernel(x, c, ctx, c_ctx, l0_w_ada, l0_b_ada, l0_norm_mix, l0_w_qkv, l0_rpb, l0_w_o, l0_norm_ffn, l0_w_gate, l0_w_up, l0_w_down, l1_w_ada, l1_b_ada, l1_norm_mix, l1_w_qkv, l1_lambda_q1, l1_lambda_k1, l1_lambda_q2, l1_lambda_k2, l1_subln, l1_w_o, l1_norm_ffn, l1_w_router, l1_w_gate, l1_w_up, l1_w_down, final_norm):
    B, S, D = x.shape
    C = ctx.shape[1]
    T, TC = B * S, B * C
    n_experts = l1_w_router.shape[1]
    tm = min(512, S)
    tmc = min(512, TC)

    xf = x.reshape(T, D)
    cf = ctx.reshape(TC, D)

    n_cond = ((B + 1 + 7) // 8) * 8
    cond = jnp.zeros((n_cond, D), F32).at[:B].set(c).at[B].set(c_ctx)
    ctx_base = B * N_MOD

    def qkv_weight(w, scale):
        col_scale = jnp.concatenate([jnp.full((D,), scale, F32), jnp.ones((2 * D,), F32)])
        return (w * col_scale[None, :]).astype(BF16)

    mod0 = ada_params(cond, l0_w_ada, l0_b_ada)
    w_qkv0 = qkv_weight(l0_w_qkv, (D // NA_HEADS) ** -0.5)
    qkv = norm_mod_qkv(xf, l0_norm_mix, mod0, 0, S, w_qkv0, tm)
    qkv_c = norm_mod_qkv(cf, l0_norm_mix, mod0, ctx_base, TC, w_qkv0, tmc)
    bias = na_bias(l0_rpb, S // GRID_W)
    o = neighbourhood_attention(qkv, qkv_c, bias, B, S, C, D)
    oc = ctx_attention(qkv_c, B, C, D)
    w_o0 = l0_w_o.astype(BF16)
    xf, h = oproj_residual_norm(o, w_o0, xf, l0_norm_ffn, mod0, 0, S, tm)
    cf, hc = oproj_residual_norm(oc, w_o0, cf, l0_norm_ffn, mod0, ctx_base, TC, tmc)
    wg0, wu0, wd0 = l0_w_gate.astype(BF16), l0_w_up.astype(BF16), l0_w_down.astype(BF16)
    d_ff = l0_w_gate.shape[1]
    tf0 = d_ff
    xf = ffn_residual(h, wg0, wu0, wd0, xf, mod0, 0, S, tm, tf0)
    cf = ffn_residual(hc, wg0, wu0, wd0, cf, mod0, ctx_base, TC, tmc, tf0)

    mod1 = ada_params(cond, l1_w_ada, l1_b_ada)
    w_qkv1 = qkv_weight(l1_w_qkv, DIFF_HEAD_DIM ** -0.5)
    qkv = norm_mod_qkv(xf, l1_norm_mix, mod1, 0, S, w_qkv1, tm, rope_tables=rope_tables(S))
    qkv_c = norm_mod_qkv(cf, l1_norm_mix, mod1, ctx_base, TC, w_qkv1, tmc)
    lam_init = 0.8 - 0.6 * math.exp(-0.3 * 1)
    lam_params = jnp.stack([l1_lambda_q1, l1_lambda_k1, l1_lambda_q2, l1_lambda_k2]).astype(F32)
    o = diff_attention(qkv, qkv_c, lam_params, l1_subln, lam_init, B, S, C, D, tq=min(1024, S))
    w_router = jnp.zeros((D, HEAD_LANES), F32).at[:, :n_experts].set(l1_w_router)
    w_router_hi = w_router.astype(BF16)
    w_router = jnp.concatenate([w_router_hi, (w_router - w_router_hi.astype(F32)).astype(BF16)], axis=1)
    xf, h, route = oproj_residual_norm(o, l1_w_o.astype(BF16), xf, l1_norm_ffn, mod1, 0, S, tm,
                                       w_router=w_router, n_experts=n_experts)
    tme = min(512, S)
    s_loc = -(-(TOP_K * tme + n_experts * (SEG_ALIGN - 1)) // HEAD_LANES) * HEAD_LANES
    plan, e1_rows, e2_rows, tile_expert, tile_valid, n_rows = routing_plan(route, n_experts, tme)
    h_sorted = dispatch_tokens(h, e1_rows, e2_rows, plan, n_rows, n_experts, tme, s_loc)
    d_ffe = l1_w_gate.shape[2]
    tfe = d_ffe // 2 if (d_ffe // 2) % 128 == 0 else d_ffe
    y_sorted = expert_ffn(h_sorted, tile_expert, tile_valid, l1_w_gate.astype(BF16), l1_w_up.astype(BF16),
                          l1_w_down.astype(BF16), tme, tfe)
    out = combine_final(y_sorted, plan, route, xf, mod1, 0, S, final_norm, n_experts, tme, s_loc)
    return out.reshape(B, S, D)
```
